```python
import math
import jax, jax.numpy as jnp
from jax import lax
import numpy as np

D_MODEL = 2048
BATCH = 1
SEQ = 8192
DEPTH = 2
DEC_BATCH = 32
DEC_SEQ = 1
PAST_LEN = 8192
PAGE_SIZE = 128

N_GROUPS = 4
GROUP_W = D_MODEL // N_GROUPS
MIX_W = N_GROUPS * GROUP_W
N_HEADS = 4
HEAD_DIM = GROUP_W // N_HEADS
DIFF_DIM = HEAD_DIM // 2
MLSTM_CHUNK = 64
MLP_CHUNK = 128
Q_BLOCK = 128
TOPK_MAX = 256
N_IDX_HEADS = 4
IDX_DIM = 64
EPS = 1e-6

PROJ_LAYOUT = (
    ('a_q', GROUP_W), ('a_k', GROUP_W), ('a_v', GROUP_W), ('a_o', GROUP_W), ('a_z', GROUP_W),
    ('a_i', N_HEADS), ('a_f', N_HEADS),
    ('b_u', GROUP_W), ('b_v', GROUP_W), ('b_z', GROUP_W),
    ('c_q', GROUP_W), ('c_k', GROUP_W), ('c_v', GROUP_W), ('c_z', GROUP_W),
    ('d_q', GROUP_W), ('d_k', GROUP_W), ('d_v', GROUP_W), ('d_z', GROUP_W),
    ('d_iq', N_IDX_HEADS * IDX_DIM), ('d_ik', IDX_DIM), ('d_iw', N_IDX_HEADS),
)
PROJ_W = sum(w for _, w in PROJ_LAYOUT)

kernel_name = 'hymba_mlstm_gmlp_diffattn_dsa_step'


def rmsnorm(x, g):
    xf = x.astype(jnp.float32)
    y = xf * lax.rsqrt(jnp.mean(xf * xf, axis=-1, keepdims=True) + EPS)
    return (y * g.astype(jnp.float32)).astype(x.dtype)


def split_proj(p):
    offs = np.cumsum([w for _, w in PROJ_LAYOUT])[:-1].tolist()
    parts = jnp.split(p, offs, axis=-1)
    return {name: part for (name, _), part in zip(PROJ_LAYOUT, parts)}


def mlstm_chunkwise(q, k, v, i_pre, f_pre, c0, n0, m0):
    B, T, H, Dh = q.shape
    L = MLSTM_CHUNK if T % MLSTM_CHUNK == 0 else T
    nc = T // L
    f32 = jnp.float32

    def to_chunks(a):
        a = a.astype(f32).reshape((B, nc, L, H) + a.shape[3:])
        return jnp.moveaxis(a, (1, 3), (0, 2))

    qc = to_chunks(q) * (Dh ** -0.5)
    kc = to_chunks(k)
    vc = to_chunks(v)
    logi = to_chunks(i_pre)
    logf = jax.nn.log_sigmoid(to_chunks(f_pre))
    causal = jnp.tril(jnp.ones((L, L), bool))

    def step(carry, xs):
        c, n, m = carry
        qb, kb, vb, li, lf = xs
        b = jnp.cumsum(lf, axis=-1)
        dmat = jnp.where(causal, b[..., :, None] - b[..., None, :] + li[..., None, :], -jnp.inf)
        inter = b + m[..., None]
        m_t = jnp.maximum(inter, jnp.max(dmat, axis=-1))
        w_inter = jnp.exp(inter - m_t)
        s = jnp.einsum('bhtd,bhsd->bhts', qb, kb) * jnp.exp(dmat - m_t[..., None])
        num = w_inter[..., None] * jnp.einsum('bhtd,bhde->bhte', qb, c) + jnp.einsum('bhts,bhse->bhte', s, vb)
        nq = w_inter * jnp.einsum('bhtd,bhd->bht', qb, n) + jnp.sum(s, axis=-1)
        den = jnp.maximum(jnp.abs(nq), jnp.exp(-m_t))
        h = num / den[..., None]
        g = b[..., -1:] - b + li
        m_new = jnp.maximum(b[..., -1] + m, jnp.max(g, axis=-1))
        ws = jnp.exp(g - m_new[..., None])
        wc = jnp.exp(b[..., -1] + m - m_new)
        c_new = wc[..., None, None] * c + jnp.einsum('bhs,bhsd,bhse->bhde', ws, kb, vb)
        n_new = wc[..., None] * n + jnp.einsum('bhs,bhsd->bhd', ws, kb)
        return (c_new, n_new, m_new), h

    (c, n, m), h = lax.scan(step, (c0.astype(f32), n0.astype(f32), m0.astype(f32)), (qc, kc, vc, logi, logf))
    h = jnp.moveaxis(h, (0, 2), (1, 3)).reshape(B, T, H, Dh)
    return h.astype(q.dtype), c.astype(c0.dtype), n.astype(n0.dtype), m.astype(m0.dtype)


def spatial_gating(u, v, w_s, b_s):
    B, T, G, dg = v.shape
    nc = -(-T // MLP_CHUNK)
    pad = nc * MLP_CHUNK - T
    vp = jnp.pad(v, ((0, 0), (0, pad), (0, 0), (0, 0))).reshape(B, nc, MLP_CHUNK, G, dg)
    w = jnp.where(jnp.tril(jnp.ones((MLP_CHUNK, MLP_CHUNK), bool)), w_s, 0.0)
    mixed = jnp.einsum('gts,bcsgd->bctgd', w, vp) + b_s.T[None, None, :, :, None]
    mixed = mixed.reshape(B, nc * MLP_CHUNK, G, dg)[:, :T]
    return (u * mixed).astype(u.dtype)


def diff_attend(q, k, v, mask, lam):
    s = jnp.einsum('bqhjd,bkhjd->bhjqk', q.astype(jnp.float32), k.astype(jnp.float32)) * (DIFF_DIM ** -0.5)
    s = jnp.where(mask[:, None, None], s, -jnp.inf)
    p = jax.nn.softmax(s, axis=-1)
    a = p[:, :, 0] - lam * p[:, :, 1]
    return jnp.einsum('bhqk,bkhd->bqhd', a, v.astype(jnp.float32))


def diff_attention_blocks(q, k, v, lam):
    B, T = q.shape[:2]
    nb = T // Q_BLOCK
    qb = jnp.moveaxis(q.reshape((B, nb, Q_BLOCK) + q.shape[2:]), 1, 0)
    kpos = jnp.arange(T)

    def blk(args):
        qi, i = args
        qpos = i * Q_BLOCK + jnp.arange(Q_BLOCK)
        return diff_attend(qi, k, v, (kpos[None, :] <= qpos[:, None])[None], lam)

    out = lax.map(blk, (qb, jnp.arange(nb)))
    return jnp.moveaxis(out, 0, 1).reshape(B, T, N_HEADS, HEAD_DIM)


def indexer_topk(q_idx, w_idx, k_idx, qpos, topk):
    s = jnp.einsum('bqhd,bld->bqhl', q_idx.astype(jnp.float32), k_idx.astype(jnp.float32)) * (IDX_DIM ** -0.5)
    score = jnp.einsum('bqh,bqhl->bql', w_idx.astype(jnp.float32) * (N_IDX_HEADS ** -0.5), jax.nn.relu(s))
    L = k_idx.shape[1]
    score = jnp.where(jnp.arange(L)[None, None, :] <= qpos[None, :, None], score, -jnp.inf)
    _, idx = lax.top_k(score, topk)
    return idx, idx <= qpos[None, :, None]


def sparse_attend(q, k_sel, v_sel, valid):
    s = jnp.einsum('bqhd,bqkhd->bhqk', q.astype(jnp.float32), k_sel.astype(jnp.float32)) * (HEAD_DIM ** -0.5)
    s = jnp.where(valid[:, None], s, -jnp.inf)
    p = jax.nn.softmax(s, axis=-1)
    return jnp.einsum('bhqk,bqkhd->bqhd', p, v_sel.astype(jnp.float32))


def take_rows(a, i):
    return jax.vmap(lambda ab, ib: ab[ib])(a, i)


def dsa_prompt(q, k, v, q_idx, k_idx, w_idx):
    B, T = q.shape[:2]
    nb = T // Q_BLOCK
    topk = min(TOPK_MAX, T // 4)
    blocks = lambda a: jnp.moveaxis(a.reshape((B, nb, Q_BLOCK) + a.shape[2:]), 1, 0)

    def blk(args):
        qb, qib, wib, i = args
        qpos = i * Q_BLOCK + jnp.arange(Q_BLOCK)
        idx, valid = indexer_topk(qib, wib, k_idx, qpos, topk)
        return sparse_attend(qb, take_rows(k, idx), take_rows(v, idx), valid)

    out = lax.map(blk, (blocks(q), blocks(q_idx), blocks(w_idx), jnp.arange(nb)))
    return jnp.moveaxis(out, 0, 1).reshape(B, T, N_HEADS, HEAD_DIM)


def dsa_sample(q, k, v, q_idx, k_idx, w_idx, pool_k, pool_v, pool_kidx, page_table):
    B, T = q.shape[:2]
    topk = min(TOPK_MAX, (PAST_LEN + T) // 4)
    kidx_all = jnp.concatenate([pool_kidx[page_table].reshape(B, PAST_LEN, IDX_DIM), k_idx], axis=1)
    qpos = PAST_LEN + jnp.arange(T)
    idx, valid = indexer_topk(q_idx, w_idx, kidx_all, qpos, topk)
    is_past = idx < PAST_LEN
    pidx = jnp.minimum(idx, PAST_LEN - 1)
    phys = jnp.take_along_axis(page_table, (pidx // PAGE_SIZE).reshape(B, -1), axis=1).reshape(idx.shape)
    off = pidx % PAGE_SIZE
    nidx = jnp.clip(idx - PAST_LEN, 0, T - 1)
    sel = lambda pool, rows: jnp.where(is_past[..., None, None], pool[phys, off], take_rows(rows, nidx))
    return sparse_attend(q, sel(pool_k, k), sel(pool_v, v), valid)


def trunk_layer(x, layer, lw, past, page_table):
    B, T, _ = x.shape
    pr = split_proj(jnp.einsum('btd,dp->btp', rmsnorm(x, lw['g_norm']), lw['w_in']))
    heads = lambda a: a.reshape(B, T, N_HEADS, -1)
    new = {}

    if past is None:
        c0 = jnp.zeros((B, N_HEADS, HEAD_DIM, HEAD_DIM), x.dtype)
        n0 = jnp.zeros((B, N_HEADS, HEAD_DIM), x.dtype)
        m0 = jnp.zeros((B, N_HEADS), x.dtype)
    else:
        c0, n0, m0 = past['mlstm_c'], past['mlstm_n'], past['mlstm_m']
    h_a, c1, n1, m1 = mlstm_chunkwise(heads(pr['a_q']), heads(pr['a_k']), heads(pr['a_v']),
                                      pr['a_i'] + lw['b_igate'], pr['a_f'] + lw['b_fgate'], c0, n0, m0)
    h_a = jax.nn.sigmoid(heads(pr['a_o'])) * h_a
    y_a = rmsnorm(h_a, lw['g_mlstm'].reshape(N_HEADS, HEAD_DIM)).reshape(B, T, GROUP_W)
    new['mlstm_c'], new['mlstm_n'], new['mlstm_m'] = c1, n1, m1

    u = jax.nn.gelu(pr['b_u'])
    v = rmsnorm(jax.nn.gelu(pr['b_v']), lw['g_gmlp'])
    y_b = spatial_gating(heads(u), heads(v), lw['w_spatial'], lw['b_spatial']).reshape(B, T, GROUP_W)
    new['gmlp_v'] = v

    qc = pr['c_q'].reshape(B, T, N_HEADS, 2, DIFF_DIM)
    kc = pr['c_k'].reshape(B, T, N_HEADS, 2, DIFF_DIM)
    vc = heads(pr['c_v'])
    lam_init = 0.8 - 0.6 * math.exp(-0.3 * layer)
    f32 = jnp.float32
    lam = (jnp.exp(jnp.sum(lw['lambda_q1'].astype(f32) * lw['lambda_k1'].astype(f32)))
           - jnp.exp(jnp.sum(lw['lambda_q2'].astype(f32) * lw['lambda_k2'].astype(f32))) + lam_init)
    if past is None:
        o_c = diff_attention_blocks(qc, kc, vc, lam)
    else:
        k_all = jnp.concatenate([past['diff_k'][page_table].reshape(B, PAST_LEN, N_HEADS, 2, DIFF_DIM), kc], axis=1)
        v_all = jnp.concatenate([past['diff_v'][page_table].reshape(B, PAST_LEN, N_HEADS, HEAD_DIM), vc], axis=1)
        qpos = PAST_LEN + jnp.arange(T)
        mask = (jnp.arange(PAST_LEN + T)[None, :] <= qpos[:, None])[None]
        o_c = diff_attend(qc, k_all, v_all, mask, lam)
    y_c = (rmsnorm(o_c, lw['g_diff']) * (1.0 - lam_init)).astype(x.dtype).reshape(B, T, GROUP_W)
    new['diff_k'], new['diff_v'] = kc.reshape(B, T, N_HEADS, HEAD_DIM), vc

    qd, kd, vd = heads(pr['d_q']), heads(pr['d_k']), heads(pr['d_v'])
    qi = pr['d_iq'].reshape(B, T, N_IDX_HEADS, IDX_DIM)
    ki, wi = pr['d_ik'], pr['d_iw']
    if past is None:
        o_d = dsa_prompt(qd, kd, vd, qi, ki, wi)
    else:
        o_d = dsa_sample(qd, kd, vd, qi, ki, wi, past['dsa_k'], past['dsa_v'], past['dsa_kidx'], page_table)
    y_d = o_d.astype(x.dtype).reshape(B, T, GROUP_W)
    new['dsa_k'], new['dsa_v'], new['dsa_kidx'] = kd, vd, ki

    mix = jnp.concatenate([y_a * jax.nn.silu(pr['a_z']), y_b * jax.nn.silu(pr['b_z']),
                           y_c * jax.nn.silu(pr['c_z']), y_d * jax.nn.silu(pr['d_z'])], axis=-1)
    return x + jnp.einsum('btm,md->btd', mix, lw['w_out']), new


def setup_inputs(seed: int = 0) -> dict:
    key = jax.random.key(seed)
    ks = jax.random.split(key, 32)
    f32 = jnp.float32
    n_pages = PAST_LEN // PAGE_SIZE
    used = DEC_BATCH * n_pages
    n_phys = used + max(1, used // 4)
    nrm = lambda k, shape, s=1.0: s * jax.random.normal(k, shape, f32)
    page_table = jax.random.permutation(ks[0], n_phys)[:used].reshape(DEC_BATCH, n_pages).astype(jnp.int32)
    kv_shape = (DEPTH, n_phys, PAGE_SIZE, N_HEADS, HEAD_DIM)
    return {
        'x_prompt': nrm(ks[1], (BATCH, SEQ, D_MODEL)),
        'x_sample': nrm(ks[2], (DEC_BATCH, DEC_SEQ, D_MODEL)),
        'state_mlstm_c': nrm(ks[3], (DEPTH, DEC_BATCH, N_HEADS, HEAD_DIM, HEAD_DIM), 0.1),
        'state_mlstm_n': nrm(ks[4], (DEPTH, DEC_BATCH, N_HEADS, HEAD_DIM), 0.1),
        'state_mlstm_m': nrm(ks[5], (DEPTH, DEC_BATCH, N_HEADS)),
        'cache_diff_k': nrm(ks[6], kv_shape),
        'cache_diff_v': nrm(ks[7], kv_shape),
        'cache_dsa_k': nrm(ks[8], kv_shape),
        'cache_dsa_v': nrm(ks[9], kv_shape),
        'cache_dsa_kidx': nrm(ks[10], (DEPTH, n_phys, PAGE_SIZE, IDX_DIM)),
        'page_table': page_table,
        'g_norm': 1.0 + nrm(ks[11], (DEPTH, D_MODEL), 0.02),
        'w_in': nrm(ks[12], (DEPTH, D_MODEL, PROJ_W), D_MODEL ** -0.5),
        'b_igate': nrm(ks[13], (DEPTH, N_HEADS), 0.1),
        'b_fgate': jnp.linspace(3.0, 6.0, N_HEADS, dtype=f32)[None] + nrm(ks[14], (DEPTH, N_HEADS), 0.1),
        'g_mlstm': 1.0 + nrm(ks[15], (DEPTH, GROUP_W), 0.02),
        'g_gmlp': 1.0 + nrm(ks[16], (DEPTH, GROUP_W), 0.02),
        'w_spatial': nrm(ks[17], (DEPTH, N_HEADS, MLP_CHUNK, MLP_CHUNK), MLP_CHUNK ** -0.5),
        'b_spatial': 1.0 + nrm(ks[18], (DEPTH, N_HEADS, MLP_CHUNK), 0.1),
        'lambda_q1': nrm(ks[19], (DEPTH, DIFF_DIM), 0.1),
        'lambda_k1': nrm(ks[20], (DEPTH, DIFF_DIM), 0.1),
        'lambda_q2': nrm(ks[21], (DEPTH, DIFF_DIM), 0.1),
        'lambda_k2': nrm(ks[22], (DEPTH, DIFF_DIM), 0.1),
        'g_diff': 1.0 + nrm(ks[23], (DEPTH, HEAD_DIM), 0.02),
        'w_out': nrm(ks[24], (DEPTH, MIX_W, D_MODEL), MIX_W ** -0.5),
        'g_final': 1.0 + nrm(ks[25], (D_MODEL,), 0.02),
    }


def reference(x_prompt, x_sample, state_mlstm_c, state_mlstm_n, state_mlstm_m,
              cache_diff_k, cache_diff_v, cache_dsa_k, cache_dsa_v, cache_dsa_kidx, page_table,
              g_norm, w_in, b_igate, b_fgate, g_mlstm, g_gmlp, w_spatial, b_spatial,
              lambda_q1, lambda_k1, lambda_q2, lambda_k2, g_diff, w_out, g_final):
    xp, xs = x_prompt, x_sample
    pn, sn = [], []
    for l in range(DEPTH):
        lw = {'g_norm': g_norm[l], 'w_in': w_in[l], 'b_igate': b_igate[l], 'b_fgate': b_fgate[l],
              'g_mlstm': g_mlstm[l], 'g_gmlp': g_gmlp[l], 'w_spatial': w_spatial[l], 'b_spatial': b_spatial[l],
              'lambda_q1': lambda_q1[l], 'lambda_k1': lambda_k1[l], 'lambda_q2': lambda_q2[l],
              'lambda_k2': lambda_k2[l], 'g_diff': g_diff[l], 'w_out': w_out[l]}
        xp, p_new = trunk_layer(xp, l, lw, None, None)
        past = {'mlstm_c': state_mlstm_c[l], 'mlstm_n': state_mlstm_n[l], 'mlstm_m': state_mlstm_m[l],
                'diff_k': cache_diff_k[l], 'diff_v': cache_diff_v[l],
                'dsa_k': cache_dsa_k[l], 'dsa_v': cache_dsa_v[l], 'dsa_kidx': cache_dsa_kidx[l]}
        xs, s_new = trunk_layer(xs, l, lw, past, page_table)
        pn.append(p_new)
        sn.append(s_new)
    st = lambda lst, name: jnp.stack([d[name] for d in lst])
    y_prompt = rmsnorm(xp, g_final)
    y_sample = rmsnorm(xs, g_final)
    return (y_prompt, y_sample,
            st(pn, 'mlstm_c'), st(pn, 'mlstm_n'), st(pn, 'mlstm_m'),
            st(pn, 'diff_k'), st(pn, 'diff_v'), st(pn, 'dsa_k'), st(pn, 'dsa_v'), st(pn, 'dsa_kidx'),
            st(sn, 'mlstm_c'), st(sn, 'mlstm_n'), st(sn, 'mlstm_m'),
            st(sn, 'diff_k'), st(sn, 'diff_v'), st(sn, 'dsa_k'), st(sn, 'dsa_v'), st(sn, 'dsa_kidx'),
            st(sn, 'gmlp_v'))
```

```python
import functools
import math

import jax
import jax.numpy as jnp
from jax import lax
from jax.experimental import pallas as pl
from jax.experimental.pallas import tpu as pltpu

F32 = jnp.float32
BF16 = jnp.bfloat16
I32 = jnp.int32

N_HEADS = 4
HEAD_DIM = 128
GROUP_W = N_HEADS * HEAD_DIM
DIFF_DIM = HEAD_DIM // 2
N_IDX_HEADS = 4
IDX_DIM = 64
TOPK_MAX = 256
EPS = 1e-6
LANES = 128
NEG_BIG = -1e30
INT_MIN = -(2 ** 31)

C_AQ, C_AK, C_AV, C_AO, C_AZ = 0, 512, 1024, 1536, 2048
C_BU, C_BV, C_BZ = 2560, 3072, 3584
C_CQ, C_CK, C_CV, C_CZ = 4096, 4608, 5120, 5632
C_DQ, C_DK, C_DV, C_DZ = 6144, 6656, 7168, 7680
C_IQ = 8192
C_TAIL = 8448
T_IW, T_AI, T_AF = 64, 68, 72
PROJ_USED = 8524
PROJ_PAD = 8704
ORIG_GATES = 2560

VMEM_LIMIT = 48 * 1024 * 1024


def _cparams(sem):
    return pltpu.CompilerParams(dimension_semantics=sem, vmem_limit_bytes=VMEM_LIMIT)


def _silu(z):
    return z * jax.nn.sigmoid(z)


def _log_sigmoid(x):
    return jnp.minimum(x, 0.0) - jnp.log1p(jnp.exp(-jnp.abs(x)))


def _nt_dot(a, b):
    return lax.dot_general(a, b, (((1,), (1,)), ((), ())), preferred_element_type=F32)


def _tn_dot(a, b):
    return lax.dot_general(a, b, (((0,), (0,)), ((), ())), preferred_element_type=F32)


def _sort_key(x):
    bits = lax.bitcast_convert_type(x + 0.0, I32)
    return bits ^ ((bits >> 31) & 0x7FFFFFFF)


def _inproj_kernel(x_ref, g_ref, w_ref, o_ref, xn_ref):
    @pl.when(pl.program_id(1) == 0)
    def _():
        x = x_ref[...]
        ms = jnp.mean(x * x, axis=-1, keepdims=True)
        xn_ref[...] = (x * lax.rsqrt(ms + EPS) * g_ref[...]).astype(BF16)

    o_ref[...] = jnp.dot(xn_ref[...], w_ref[...], preferred_element_type=F32)


def _inproj(x, g, w, tm):
    m, d = x.shape
    tn = 512
    return pl.pallas_call(
        _inproj_kernel,
        grid=(m // tm, PROJ_PAD // tn),
        in_specs=[pl.BlockSpec((tm, d), lambda i, j: (i, 0)),
                  pl.BlockSpec((1, d), lambda i, j: (0, 0)),
                  pl.BlockSpec((d, tn), lambda i, j: (0, j))],
        out_specs=pl.BlockSpec((tm, tn), lambda i, j: (i, j)),
        out_shape=jax.ShapeDtypeStruct((m, PROJ_PAD), F32),
        scratch_shapes=[pltpu.VMEM((tm, d), BF16)],
        compiler_params=_cparams(("parallel", "arbitrary")),
        name="inproj",
    )(x, g, w)


def _outproj_kernel(x_ref, a_ref, b_ref, c_ref, d_ref, w_ref, gf_ref, o_ref, *, final):
    acc = x_ref[...]
    for i, m_ref in enumerate((a_ref, b_ref, c_ref, d_ref)):
        acc = acc + jnp.dot(m_ref[...], w_ref[i * GROUP_W:(i + 1) * GROUP_W, :], preferred_element_type=F32)
    if final:
        ms = jnp.mean(acc * acc, axis=-1, keepdims=True)
        acc = acc * lax.rsqrt(ms + EPS) * gf_ref[...]
    o_ref[...] = acc


def _outproj(x, mixes, w, gf, tm, final):
    m, d = x.shape
    row = lambda i: (i, 0)
    const = lambda i: (0, 0)
    return pl.pallas_call(
        functools.partial(_outproj_kernel, final=final),
        grid=(m // tm,),
        in_specs=[pl.BlockSpec((tm, d), row)] + [pl.BlockSpec((tm, GROUP_W), row)] * 4
                 + [pl.BlockSpec((4 * GROUP_W, d), const), pl.BlockSpec((1, d), const)],
        out_specs=pl.BlockSpec((tm, d), row),
        out_shape=jax.ShapeDtypeStruct((m, d), F32),
        compiler_params=_cparams(("parallel",)),
        name="outproj",
    )(x, *mixes, w, gf)


def _mlstm_kernel(q_ref, k_ref, v_ref, o_ref, z_ref, g_ref, gt_ref, bi_ref, bf_ref, gm_ref,
                  y_ref, c_ref, n_ref, m_ref, *, L):
    h = pl.program_id(0)

    @pl.when(pl.program_id(1) == 0)
    def _():
        c_ref[...] = jnp.zeros_like(c_ref)
        n_ref[...] = jnp.zeros_like(n_ref)
        m_ref[...] = jnp.zeros_like(m_ref)

    bi = bi_ref[h]
    bf = bf_ref[h]
    q = q_ref[...] * (HEAD_DIM ** -0.5)
    k = k_ref[...]
    qb = q.astype(BF16)
    kb = k.astype(BF16)
    vb = v_ref[...].astype(BF16)

    li_r = gt_ref[pl.ds(h, 1), :] + bi
    lf_r = _log_sigmoid(gt_ref[pl.ds(h + N_HEADS, 1), :] + bf)
    g = g_ref[...]
    lane8 = lax.broadcasted_iota(I32, g.shape, 1)
    li_c = jnp.sum(jnp.where(lane8 == h, g, 0.0), axis=1, keepdims=True) + bi
    lf_c = _log_sigmoid(jnp.sum(jnp.where(lane8 == h + N_HEADS, g, 0.0), axis=1, keepdims=True) + bf)

    row = lax.broadcasted_iota(I32, (L, L), 0)
    col = lax.broadcasted_iota(I32, (L, L), 1)
    causal = col <= row
    b_c = jnp.sum(jnp.where(causal, lf_r, 0.0), axis=1, keepdims=True)
    b_r = jnp.sum(jnp.where(row <= col, lf_c, 0.0), axis=0, keepdims=True)
    b_last = jnp.sum(lf_r, axis=1, keepdims=True)

    m0 = m_ref[:, 0:1]
    c0 = c_ref[...]
    n0 = n_ref[...]
    dmat = jnp.where(causal, b_c - b_r + li_r, -jnp.inf)
    inter = b_c + m0
    m_t = jnp.maximum(inter, jnp.max(dmat, axis=1, keepdims=True))
    w_inter = jnp.exp(inter - m_t)
    s = _nt_dot(qb, kb) * jnp.exp(dmat - m_t)
    num = (w_inter * jnp.dot(qb, c0.astype(BF16), preferred_element_type=F32)
           + jnp.dot(s.astype(BF16), vb, preferred_element_type=F32))
    nq = w_inter * jnp.sum(q * n0, axis=1, keepdims=True) + jnp.sum(s, axis=1, keepdims=True)
    den = jnp.maximum(jnp.abs(nq), jnp.exp(-m_t))
    hh = num / den

    g_r = b_last - b_r + li_r
    g_c = b_last - b_c + li_c
    m_new = jnp.maximum(b_last + m0, jnp.max(g_r, axis=1, keepdims=True))
    ws_c = jnp.exp(g_c - m_new)
    wc = jnp.exp(b_last + m0 - m_new)
    kw = k * ws_c
    c_ref[...] = wc * c0 + _tn_dot(kw.astype(BF16), vb)
    n_ref[...] = wc * n0 + jnp.sum(kw, axis=0, keepdims=True)
    m_ref[...] = jnp.broadcast_to(m_new, m_ref.shape)

    ha = jax.nn.sigmoid(o_ref[...]) * hh
    y = ha * lax.rsqrt(jnp.mean(ha * ha, axis=1, keepdims=True) + EPS) * gm_ref[...]
    y_ref[...] = (y * _silu(z_ref[...])).astype(BF16)


def _mlstm_prompt(proj, gates, gates_t, b_i, b_f, g_mlstm, L):
    t = proj.shape[0]
    hb = lambda base: (lambda h, c: (c, base // HEAD_DIM + h))
    smem = pl.BlockSpec(memory_space=pltpu.SMEM)
    return pl.pallas_call(
        functools.partial(_mlstm_kernel, L=L),
        grid=(N_HEADS, t // L),
        in_specs=[pl.BlockSpec((L, HEAD_DIM), hb(C_AQ)), pl.BlockSpec((L, HEAD_DIM), hb(C_AK)),
                  pl.BlockSpec((L, HEAD_DIM), hb(C_AV)), pl.BlockSpec((L, HEAD_DIM), hb(C_AO)),
                  pl.BlockSpec((L, HEAD_DIM), hb(C_AZ)),
                  pl.BlockSpec((L, 2 * N_HEADS), lambda h, c: (c, 0)),
                  pl.BlockSpec((2 * N_HEADS, L), lambda h, c: (0, c)),
                  smem, smem,
                  pl.BlockSpec((1, HEAD_DIM), lambda h, c: (0, h))],
        out_specs=[pl.BlockSpec((L, HEAD_DIM), lambda h, c: (c, h)),
                   pl.BlockSpec((None, HEAD_DIM, HEAD_DIM), lambda h, c: (h, 0, 0)),
                   pl.BlockSpec((None, 1, HEAD_DIM), lambda h, c: (h, 0, 0)),
                   pl.BlockSpec((None, 1, LANES), lambda h, c: (h, 0, 0))],
        out_shape=[jax.ShapeDtypeStruct((t, GROUP_W), BF16),
                   jax.ShapeDtypeStruct((N_HEADS, HEAD_DIM, HEAD_DIM), F32),
                   jax.ShapeDtypeStruct((N_HEADS, 1, HEAD_DIM), F32),
                   jax.ShapeDtypeStruct((N_HEADS, 1, LANES), F32)],
        compiler_params=_cparams(("parallel", "arbitrary")),
        name="mlstm_prompt",
    )(proj, proj, proj, proj, proj, gates, gates_t, b_i, b_f, g_mlstm)


def _gmlp_kernel(u_ref, v_ref, z_ref, gg_ref, w_ref, bs_ref, y_ref, *, C):
    u = jax.nn.gelu(u_ref[...])
    gv = jax.nn.gelu(v_ref[...])
    v = gv * lax.rsqrt(jnp.mean(gv * gv, axis=1, keepdims=True) + EPS) * gg_ref[...]
    sz = _silu(z_ref[...])
    row = lax.broadcasted_iota(I32, (C, C), 0)
    col = lax.broadcasted_iota(I32, (C, C), 1)
    for g in range(N_HEADS):
        w = jnp.where(col <= row, w_ref[g], 0.0).astype(BF16)
        bcol = bs_ref[:, g:g + 1]
        gs = slice(g * HEAD_DIM, (g + 1) * HEAD_DIM)
        for r in range(u.shape[0] // C):
            rs = slice(r * C, (r + 1) * C)
            mixed = jnp.dot(w, v[rs, gs].astype(BF16), preferred_element_type=F32) + bcol
            y_ref[rs, gs] = (u[rs, gs] * mixed * sz[rs, gs]).astype(BF16)


def _gmlp_prompt(proj, g_gmlp, w_spatial, b_spatial_t, tr):
    t = proj.shape[0]
    c = w_spatial.shape[-1]
    blk = lambda base: pl.BlockSpec((tr, GROUP_W), lambda i: (i, base // GROUP_W))
    return pl.pallas_call(
        functools.partial(_gmlp_kernel, C=c),
        grid=(t // tr,),
        in_specs=[blk(C_BU), blk(C_BV), blk(C_BZ),
                  pl.BlockSpec((1, GROUP_W), lambda i: (0, 0)),
                  pl.BlockSpec((N_HEADS, c, c), lambda i: (0, 0, 0)),
                  pl.BlockSpec((c, N_HEADS), lambda i: (0, 0))],
        out_specs=pl.BlockSpec((tr, GROUP_W), lambda i: (i, 0)),
        out_shape=jax.ShapeDtypeStruct((t, GROUP_W), BF16),
        compiler_params=_cparams(("parallel",)),
        name="gmlp_prompt",
    )(proj, proj, proj, g_gmlp, w_spatial, b_spatial_t)


def _lambda(lam_ref, lam_init):
    lp = lam_ref[...]
    s1 = jnp.sum(lp[0:1] * lp[1:2], axis=1, keepdims=True)
    s2 = jnp.sum(lp[2:3] * lp[3:4], axis=1, keepdims=True)
    return jnp.exp(s1) - jnp.exp(s2) + lam_init


def _diff_finish(o, gd, z, lam_init):
    y = o * lax.rsqrt(jnp.mean(o * o, axis=1, keepdims=True) + EPS) * gd
    return (y * (1.0 - lam_init)) * _silu(z)


def _diff_kernel(q_ref, k_ref, v_ref, z_ref, lam_ref, gd_ref, y_ref,
                 q0_s, q1_s, m_s, l_s, acc_s, *, lam_init, tq, tk):
    qi = pl.program_id(1)
    ki = pl.program_id(2)

    @pl.when(ki == 0)
    def _():
        q = q_ref[...]
        lane = lax.broadcasted_iota(I32, q.shape, 1)
        q0_s[...] = jnp.where(lane < DIFF_DIM, q, 0.0).astype(BF16)
        q1_s[...] = jnp.where(lane >= DIFF_DIM, q, 0.0).astype(BF16)
        m_s[...] = jnp.full_like(m_s, NEG_BIG)
        l_s[...] = jnp.zeros_like(l_s)
        acc_s[...] = jnp.zeros_like(acc_s)

    @pl.when(ki <= qi)
    def _():
        kb = k_ref[...].astype(BF16)
        vb = v_ref[...].astype(BF16)
        row = qi * tq + lax.broadcasted_iota(I32, (tq, tk), 0)
        col = ki * tk + lax.broadcasted_iota(I32, (tq, tk), 1)
        msk = col <= row
        for j, qs in enumerate((q0_s, q1_s)):
            s = jnp.where(msk, _nt_dot(qs[...], kb) * (DIFF_DIM ** -0.5), NEG_BIG)
            m_old = m_s[j]
            m_new = jnp.maximum(m_old, jnp.max(s, axis=1, keepdims=True))
            p = jnp.where(msk, jnp.exp(s - m_new), 0.0)
            alpha = jnp.exp(m_old - m_new)
            l_s[j] = alpha * l_s[j] + jnp.sum(p, axis=1, keepdims=True)
            acc_s[j] = alpha * acc_s[j] + jnp.dot(p.astype(BF16), vb, preferred_element_type=F32)
            m_s[j] = m_new

    @pl.when(ki == pl.num_programs(2) - 1)
    def _():
        lam = _lambda(lam_ref, lam_init)
        o = acc_s[0] / l_s[0] - lam * (acc_s[1] / l_s[1])
        y_ref[...] = _diff_finish(o, gd_ref[...], z_ref[...], lam_init).astype(BF16)


def _diff_prompt(proj, lam_p, g_diff, lam_init, tq, tk):
    t = proj.shape[0]
    nq, nk = t // tq, t // tk
    kv = lambda base: pl.BlockSpec((tk, HEAD_DIM), lambda h, i, j: (jnp.minimum(j, i), base // HEAD_DIM + h))
    qz = lambda base: pl.BlockSpec((tq, HEAD_DIM), lambda h, i, j: (i, base // HEAD_DIM + h))
    return pl.pallas_call(
        functools.partial(_diff_kernel, lam_init=lam_init, tq=tq, tk=tk),
        grid=(N_HEADS, nq, nk),
        in_specs=[qz(C_CQ), kv(C_CK), kv(C_CV), qz(C_CZ),
                  pl.BlockSpec((4, DIFF_DIM), lambda h, i, j: (0, 0)),
                  pl.BlockSpec((1, HEAD_DIM), lambda h, i, j: (0, 0))],
        out_specs=pl.BlockSpec((tq, HEAD_DIM), lambda h, i, j: (i, h)),
        out_shape=jax.ShapeDtypeStruct((t, GROUP_W), BF16),
        scratch_shapes=[pltpu.VMEM((tq, HEAD_DIM), BF16), pltpu.VMEM((tq, HEAD_DIM), BF16),
                        pltpu.VMEM((2, tq, 1), F32), pltpu.VMEM((2, tq, 1), F32),
                        pltpu.VMEM((2, tq, HEAD_DIM), F32)],
        compiler_params=_cparams(("parallel", "parallel", "arbitrary")),
        name="diff_prompt",
    )(proj, proj, proj, proj, lam_p, g_diff)


def _kth_largest(count_ge, shape, kk):
    thr = jnp.where(count_ge(jnp.zeros(shape, I32)) >= kk, 0, INT_MIN).astype(I32)

    def bit_step(i, thr):
        cand = thr | (jnp.int32(1) << (30 - i))
        return jnp.where(count_ge(cand) >= kk, cand, thr)

    return lax.fori_loop(0, 31, bit_step, thr)


def _dsa_kernel(q_ref, k_ref, v_ref, z_ref, iq_ref, tail_ref, kt_ref, y_ref,
                keys_s, thr_s, need_s, run_s, m_s, l_s, acc_s, *, tq, topk, rg):
    qi = pl.program_id(0)
    ki = pl.program_id(1)
    tk = tq

    @pl.when(ki == 0)
    def _():
        iq = iq_ref[...]
        grp = lax.broadcasted_iota(I32, iq.shape, 1) // IDX_DIM
        qh = [jnp.where(grp == h, iq, 0.0).astype(BF16) for h in range(N_IDX_HEADS)]
        tail = tail_ref[...]
        wh = [tail[:, T_IW + h:T_IW + h + 1] * (N_IDX_HEADS ** -0.5) for h in range(N_IDX_HEADS)]
        rowg = qi * tq + lax.broadcasted_iota(I32, (tq, tk), 0)
        col0 = lax.broadcasted_iota(I32, (tq, tk), 1)

        def score_chunk(c, carry):
            off = pl.multiple_of(c * tk, tk)
            kt = kt_ref[:, pl.ds(off, tk)]
            sc = jnp.zeros((tq, tk), F32)
            for h in range(N_IDX_HEADS):
                sh = jnp.dot(qh[h], kt, preferred_element_type=F32) * (IDX_DIM ** -0.5)
                sc = sc + wh[h] * jnp.maximum(sh, 0.0)
            keys_s[:, pl.ds(off, tk)] = jnp.where(c * tk + col0 <= rowg, _sort_key(sc), INT_MIN)
            return carry

        lax.fori_loop(0, qi + 1, score_chunk, 0)

        def search_rows(g, carry):
            r0 = pl.multiple_of(g * rg, rg)

            def count(pred):
                def body(c, acc):
                    kk = keys_s[pl.ds(r0, rg), pl.ds(pl.multiple_of(c * tk, tk), tk)]
                    hit = jnp.where(pred(kk), 1, 0).astype(I32)
                    for j in range(tk // LANES):
                        acc = acc + hit[:, j * LANES:(j + 1) * LANES]
                    return acc
                acc = lax.fori_loop(0, qi + 1, body, jnp.zeros((rg, LANES), I32))
                return jnp.sum(acc, axis=1, keepdims=True)

            thr = _kth_largest(lambda cand: count(lambda kk: kk >= cand), (rg, 1), topk)
            thr_s[pl.ds(r0, rg), :] = thr
            need_s[pl.ds(r0, rg), :] = (topk - count(lambda kk: kk > thr)).astype(F32)
            return carry

        lax.fori_loop(0, tq // rg, search_rows, 0)
        run_s[...] = jnp.zeros_like(run_s)
        m_s[...] = jnp.full_like(m_s, NEG_BIG)
        l_s[...] = jnp.zeros_like(l_s)
        acc_s[...] = jnp.zeros_like(acc_s)

    @pl.when(ki <= qi)
    def _():
        key = keys_s[:, pl.ds(pl.multiple_of(ki * tk, tk), tk)]
        thr = thr_s[...]
        eq = key == thr
        eqb = jnp.where(eq, 1.0, 0.0).astype(BF16)
        upper = jnp.where(lax.broadcasted_iota(I32, (tk, tk), 0) <= lax.broadcasted_iota(I32, (tk, tk), 1),
                          1.0, 0.0).astype(BF16)
        rank = run_s[...] + jnp.dot(eqb, upper, preferred_element_type=F32)
        run_s[...] = rank[:, tk - 1:tk]
        row = qi * tq + lax.broadcasted_iota(I32, (tq, tk), 0)
        col = ki * tk + lax.broadcasted_iota(I32, (tq, tk), 1)
        sel = ((key > thr) | (eq & (rank <= need_s[...]))) & (col <= row)
        q = q_ref[...].astype(BF16)
        kb = k_ref[...].astype(BF16)
        vb = v_ref[...].astype(BF16)
        for h in range(N_HEADS):
            hs = slice(h * HEAD_DIM, (h + 1) * HEAD_DIM)
            s = jnp.where(sel, _nt_dot(q[:, hs], kb[:, hs]) * (HEAD_DIM ** -0.5), NEG_BIG)
            m_old = m_s[h]
            m_new = jnp.maximum(m_old, jnp.max(s, axis=1, keepdims=True))
            p = jnp.where(sel, jnp.exp(s - m_new), 0.0)
            alpha = jnp.exp(m_old - m_new)
            l_s[h] = alpha * l_s[h] + jnp.sum(p, axis=1, keepdims=True)
            acc_s[:, hs] = alpha * acc_s[:, hs] + jnp.dot(p.astype(BF16), vb[:, hs], preferred_element_type=F32)
            m_s[h] = m_new

    @pl.when(ki == pl.num_programs(1) - 1)
    def _():
        sz = _silu(z_ref[...])
        for h in range(N_HEADS):
            hs = slice(h * HEAD_DIM, (h + 1) * HEAD_DIM)
            y_ref[:, hs] = ((acc_s[:, hs] / l_s[h]) * sz[:, hs]).astype(BF16)


def _dsa_prompt(proj, kidx_t4, tq, rg):
    t = proj.shape[0]
    n = t // tq
    topk = min(TOPK_MAX, t // 4)
    qblk = lambda base, w: pl.BlockSpec((tq, w), lambda i, j: (i, base // w))
    kblk = lambda base: pl.BlockSpec((tq, GROUP_W), lambda i, j: (jnp.minimum(j, i), base // GROUP_W))
    return pl.pallas_call(
        functools.partial(_dsa_kernel, tq=tq, topk=topk, rg=rg),
        grid=(n, n),
        in_specs=[qblk(C_DQ, GROUP_W), kblk(C_DK), kblk(C_DV), qblk(C_DZ, GROUP_W),
                  qblk(C_IQ, N_IDX_HEADS * IDX_DIM), qblk(C_TAIL, LANES),
                  pl.BlockSpec((N_IDX_HEADS * IDX_DIM, t), lambda i, j: (0, 0))],
        out_specs=pl.BlockSpec((tq, GROUP_W), lambda i, j: (i, 0)),
        out_shape=jax.ShapeDtypeStruct((t, GROUP_W), BF16),
        scratch_shapes=[pltpu.VMEM((tq, t), I32), pltpu.VMEM((tq, 1), I32), pltpu.VMEM((tq, 1), F32),
                        pltpu.VMEM((tq, 1), F32), pltpu.VMEM((N_HEADS, tq, 1), F32),
                        pltpu.VMEM((N_HEADS, tq, 1), F32), pltpu.VMEM((tq, GROUP_W), F32)],
        compiler_params=_cparams(("parallel", "arbitrary")),
        name="dsa_prompt",
    )(proj, proj, proj, proj, proj, proj, kidx_t4)


def _sample_small_kernel(proj_ref, c0_ref, n0_ref, m0_ref, bi_ref, bf_ref, gm_ref, gg_ref, w00_ref, b0_ref,
                         ya_ref, yb_ref, gv_ref, c1_ref, n1_ref, m1_ref):
    b = pl.program_id(0)
    rowv = proj_ref[pl.ds(b, 1), :]
    eye = lax.broadcasted_iota(I32, (HEAD_DIM, HEAD_DIM), 0) == lax.broadcasted_iota(I32, (HEAD_DIM, HEAD_DIM), 1)
    for h in range(N_HEADS):
        hs = lambda base: slice(base + h * HEAD_DIM, base + (h + 1) * HEAD_DIM)
        q = rowv[:, hs(C_AQ)] * (HEAD_DIM ** -0.5)
        k = rowv[:, hs(C_AK)]
        v = rowv[:, hs(C_AV)]
        li = rowv[:, C_TAIL + T_AI + h:C_TAIL + T_AI + h + 1] + bi_ref[h]
        lf = _log_sigmoid(rowv[:, C_TAIL + T_AF + h:C_TAIL + T_AF + h + 1] + bf_ref[h])
        c0 = c0_ref[h]
        n0 = n0_ref[h]
        m0 = m0_ref[h][:, 0:1]
        inter = lf + m0
        m_t = jnp.maximum(inter, li)
        w_inter = jnp.exp(inter - m_t)
        w_new = jnp.exp(li - m_t)
        s = jnp.sum(q * k, axis=1, keepdims=True) * w_new
        qc = jnp.dot(q.astype(BF16), c0.astype(BF16), preferred_element_type=F32)
        num = w_inter * qc + s * v
        nq = w_inter * jnp.sum(q * n0, axis=1, keepdims=True) + s
        den = jnp.maximum(jnp.abs(nq), jnp.exp(-m_t))
        hh = num / den
        k_col = jnp.sum(jnp.where(eye, k, 0.0), axis=1, keepdims=True)
        c1_ref[h] = w_inter * c0 + w_new * (k_col * v)
        n1_ref[h] = w_inter * n0 + w_new * k
        m1_ref[h] = jnp.broadcast_to(m_t, (1, LANES))
        ha = jax.nn.sigmoid(rowv[:, hs(C_AO)]) * hh
        y = ha * lax.rsqrt(jnp.mean(ha * ha, axis=1, keepdims=True) + EPS) * gm_ref[:, h * HEAD_DIM:(h + 1) * HEAD_DIM]
        ya_ref[:, h * HEAD_DIM:(h + 1) * HEAD_DIM] = (y * _silu(rowv[:, hs(C_AZ)])).astype(BF16)

    u = jax.nn.gelu(rowv[:, C_BU:C_BU + GROUP_W])
    gv = jax.nn.gelu(rowv[:, C_BV:C_BV + GROUP_W])
    vv = gv * lax.rsqrt(jnp.mean(gv * gv, axis=1, keepdims=True) + EPS) * gg_ref[...]
    gv_ref[...] = vv
    sz = _silu(rowv[:, C_BZ:C_BZ + GROUP_W])
    for g in range(N_HEADS):
        gs = slice(g * HEAD_DIM, (g + 1) * HEAD_DIM)
        mixed = w00_ref[g] * vv[:, gs] + b0_ref[g]
        yb_ref[:, gs] = (u[:, gs] * mixed * sz[:, gs]).astype(BF16)


def _sample_small(proj, c0, n0, m0, b_i, b_f, g_mlstm, g_gmlp, w00, b0):
    nb = proj.shape[0]
    smem = pl.BlockSpec(memory_space=pltpu.SMEM)
    per_b = lambda *tail: pl.BlockSpec((None,) + tail, lambda b: (b,) + (0,) * len(tail))
    row_out = lambda dt: jax.ShapeDtypeStruct((nb, 1, GROUP_W), dt)
    return pl.pallas_call(
        _sample_small_kernel,
        grid=(nb,),
        in_specs=[pl.BlockSpec(proj.shape, lambda b: (0, 0)),
                  per_b(N_HEADS, HEAD_DIM, HEAD_DIM), per_b(N_HEADS, 1, HEAD_DIM), per_b(N_HEADS, 1, LANES),
                  smem, smem,
                  pl.BlockSpec((1, GROUP_W), lambda b: (0, 0)), pl.BlockSpec((1, GROUP_W), lambda b: (0, 0)),
                  smem, smem],
        out_specs=[per_b(1, GROUP_W), per_b(1, GROUP_W), per_b(1, GROUP_W),
                   per_b(N_HEADS, HEAD_DIM, HEAD_DIM), per_b(N_HEADS, 1, HEAD_DIM), per_b(N_HEADS, 1, LANES)],
        out_shape=[row_out(BF16), row_out(BF16), row_out(F32),
                   jax.ShapeDtypeStruct((nb, N_HEADS, HEAD_DIM, HEAD_DIM), F32),
                   jax.ShapeDtypeStruct((nb, N_HEADS, 1, HEAD_DIM), F32),
                   jax.ShapeDtypeStruct((nb, N_HEADS, 1, LANES), F32)],
        compiler_params=_cparams(("parallel",)),
        name="sample_small",
    )(proj, c0, n0, m0, b_i, b_f, g_mlstm, g_gmlp, w00, b0)


def _pad_rows(x, n):
    return jnp.concatenate([x, jnp.zeros((n - x.shape[0], x.shape[1]), x.dtype)], axis=0)


def _sample_diff_kernel(pt_ref, proj_ref, k_ref, v_ref, lam_ref, gd_ref, y_ref, m_s, l_s, acc_s, *, lam_init):
    b = pl.program_id(0)
    p = pl.program_id(1)
    rowv = proj_ref[pl.ds(b, 1), :]
    lane = lax.broadcasted_iota(I32, (1, HEAD_DIM), 1)
    q2 = []
    for h in range(N_HEADS):
        qh = rowv[:, C_CQ + h * HEAD_DIM:C_CQ + (h + 1) * HEAD_DIM]
        q2.append(jnp.concatenate([jnp.where(lane < DIFF_DIM, qh, 0.0), jnp.where(lane >= DIFF_DIM, qh, 0.0)], axis=0))

    @pl.when(p == 0)
    def _():
        for h in range(N_HEADS):
            kn = rowv[:, C_CK + h * HEAD_DIM:C_CK + (h + 1) * HEAD_DIM]
            vn = rowv[:, C_CV + h * HEAD_DIM:C_CV + (h + 1) * HEAD_DIM]
            m_s[h] = jnp.sum(q2[h] * kn, axis=1, keepdims=True) * (DIFF_DIM ** -0.5)
            l_s[h] = jnp.ones((2, 1), F32)
            acc_s[h] = jnp.broadcast_to(vn, (2, HEAD_DIM))

    kb = k_ref[...].astype(BF16)
    vb = v_ref[...].astype(BF16)
    for h in range(N_HEADS):
        hs = slice(h * HEAD_DIM, (h + 1) * HEAD_DIM)
        s = _nt_dot(_pad_rows(q2[h], 8).astype(BF16), kb[:, hs])[0:2] * (DIFF_DIM ** -0.5)
        m_old = m_s[h]
        m_new = jnp.maximum(m_old, jnp.max(s, axis=1, keepdims=True))
        pr = jnp.exp(s - m_new)
        alpha = jnp.exp(m_old - m_new)
        l_s[h] = alpha * l_s[h] + jnp.sum(pr, axis=1, keepdims=True)
        pv = jnp.dot(_pad_rows(pr, 8).astype(BF16), vb[:, hs], preferred_element_type=F32)[0:2]
        acc_s[h] = alpha * acc_s[h] + pv
        m_s[h] = m_new

    @pl.when(p == pl.num_programs(1) - 1)
    def _():
        lam = _lambda(lam_ref, lam_init)
        for h in range(N_HEADS):
            a = acc_s[h] / l_s[h]
            o = a[0:1] - lam * a[1:2]
            z = rowv[:, C_CZ + h * HEAD_DIM:C_CZ + (h + 1) * HEAD_DIM]
            y_ref[:, h * HEAD_DIM:(h + 1) * HEAD_DIM] = _diff_finish(o, gd_ref[...], z, lam_init).astype(BF16)


def _sample_diff(page_table, proj, cache_k, cache_v, layer, lam_p, g_diff, lam_init):
    nb, n_pages = page_table.shape
    page = cache_k.shape[2]
    pblk = pl.BlockSpec((None, None, page, GROUP_W), lambda b, p, pt: (layer, pt[b, p], 0, 0))
    grid_spec = pltpu.PrefetchScalarGridSpec(
        num_scalar_prefetch=1,
        grid=(nb, n_pages),
        in_specs=[pl.BlockSpec(proj.shape, lambda b, p, pt: (0, 0)), pblk, pblk,
                  pl.BlockSpec((4, DIFF_DIM), lambda b, p, pt: (0, 0)),
                  pl.BlockSpec((1, HEAD_DIM), lambda b, p, pt: (0, 0))],
        out_specs=pl.BlockSpec((None, 1, GROUP_W), lambda b, p, pt: (b, 0, 0)),
        scratch_shapes=[pltpu.VMEM((N_HEADS, 2, 1), F32), pltpu.VMEM((N_HEADS, 2, 1), F32),
                        pltpu.VMEM((N_HEADS, 2, HEAD_DIM), F32)],
    )
    return pl.pallas_call(
        functools.partial(_sample_diff_kernel, lam_init=lam_init),
        grid_spec=grid_spec,
        out_shape=jax.ShapeDtypeStruct((nb, 1, GROUP_W), BF16),
        compiler_params=_cparams(("parallel", "arbitrary")),
        name="sample_diff",
    )(page_table, proj, cache_k, cache_v, lam_p, g_diff)


def _sample_idx_heads(rowv):
    q = [rowv[:, C_IQ + h * IDX_DIM:C_IQ + (h + 1) * IDX_DIM] for h in range(N_IDX_HEADS)]
    w = [rowv[:, C_TAIL + T_IW + h:C_TAIL + T_IW + h + 1] * (N_IDX_HEADS ** -0.5) for h in range(N_IDX_HEADS)]
    return q, w


def _sample_score_kernel(pt_ref, proj_ref, ki_ref, keys_ref, thr_ref, need_ref, *, topk, n_pages):
    b = pl.program_id(0)
    p = pl.program_id(1)
    rowv = proj_ref[pl.ds(b, 1), :]
    qs, ws = _sample_idx_heads(rowv)

    @pl.when(p == 0)
    def _():
        keys_ref[...] = jnp.full(keys_ref.shape, INT_MIN, I32)

    qm = _pad_rows(jnp.concatenate(qs, axis=0), 8).astype(BF16)
    sh = _nt_dot(qm, ki_ref[...].astype(BF16)) * (IDX_DIM ** -0.5)
    sc = jnp.zeros((1, sh.shape[1]), F32)
    for h in range(N_IDX_HEADS):
        sc = sc + ws[h] * jnp.maximum(sh[h:h + 1], 0.0)
    keys_ref[pl.ds(p, 1), :] = _sort_key(sc)

    @pl.when(p == n_pages - 1)
    def _():
        kn = rowv[:, C_TAIL:C_TAIL + IDX_DIM]
        sn = jnp.zeros((1, 1), F32)
        for h in range(N_IDX_HEADS):
            dh = jnp.sum(qs[h] * kn, axis=1, keepdims=True)
            sn = sn + ws[h] * jnp.maximum(dh * (IDX_DIM ** -0.5), 0.0)
        lane = lax.broadcasted_iota(I32, (1, keys_ref.shape[1]), 1)
        keys_ref[pl.ds(n_pages, 1), :] = jnp.where(lane == 0, _sort_key(sn), INT_MIN)

        keys = keys_ref[...]

        def count(pred):
            hit = jnp.where(pred(keys), 1, 0).astype(I32)
            return jnp.sum(jnp.sum(hit, axis=0, keepdims=True), axis=1, keepdims=True)

        thr = _kth_largest(lambda cand: count(lambda kk: kk >= cand), (1, 1), topk)
        thr_ref[...] = jnp.broadcast_to(thr, thr_ref.shape)
        need_ref[...] = jnp.broadcast_to((topk - count(lambda kk: kk > thr)).astype(F32), need_ref.shape)


def _sample_scores(page_table, proj, cache_kidx, layer, topk):
    nb, n_pages = page_table.shape
    page = cache_kidx.shape[2]
    rows = -(-(n_pages + 1) // 8) * 8
    grid_spec = pltpu.PrefetchScalarGridSpec(
        num_scalar_prefetch=1,
        grid=(nb, n_pages),
        in_specs=[pl.BlockSpec(proj.shape, lambda b, p, pt: (0, 0)),
                  pl.BlockSpec((None, None, page, IDX_DIM), lambda b, p, pt: (layer, pt[b, p], 0, 0))],
        out_specs=[pl.BlockSpec((None, rows, page), lambda b, p, pt: (b, 0, 0)),
                   pl.BlockSpec((None, 1, LANES), lambda b, p, pt: (b, 0, 0)),
                   pl.BlockSpec((None, 1, LANES), lambda b, p, pt: (b, 0, 0))],
    )
    return pl.pallas_call(
        functools.partial(_sample_score_kernel, topk=topk, n_pages=n_pages),
        grid_spec=grid_spec,
        out_shape=[jax.ShapeDtypeStruct((nb, rows, page), I32),
                   jax.ShapeDtypeStruct((nb, 1, LANES), I32),
                   jax.ShapeDtypeStruct((nb, 1, LANES), F32)],
        compiler_params=_cparams(("parallel", "arbitrary")),
        name="sample_scores",
    )(page_table, proj, cache_kidx)


def _sample_dsa_kernel(pt_ref, proj_ref, k_ref, v_ref, keys_ref, thr_ref, need_ref, y_ref,
                       run_s, m_s, l_s, acc_s, *, n_pages):
    b = pl.program_id(0)
    p = pl.program_id(1)
    rowv = proj_ref[pl.ds(b, 1), :]
    page = k_ref.shape[0]
    thr = thr_ref[:, 0:1]
    need = need_ref[:, 0:1]

    @pl.when(p == 0)
    def _():
        run_s[...] = jnp.zeros_like(run_s)
        m_s[...] = jnp.full_like(m_s, NEG_BIG)
        l_s[...] = jnp.zeros_like(l_s)
        acc_s[...] = jnp.zeros_like(acc_s)

    def update(h, s, sel, vals):
        s = jnp.where(sel, s, NEG_BIG)
        m_old = m_s[h]
        m_new = jnp.maximum(m_old, jnp.max(s, axis=1, keepdims=True))
        pr = jnp.where(sel, jnp.exp(s - m_new), 0.0)
        alpha = jnp.exp(m_old - m_new)
        l_s[h] = alpha * l_s[h] + jnp.sum(pr, axis=1, keepdims=True)
        acc_s[h] = alpha * acc_s[h] + vals(pr)
        m_s[h] = m_new

    key = keys_ref[pl.ds(p, 1), :]
    eq = key == thr
    eqb = _pad_rows(jnp.where(eq, 1.0, 0.0), 8).astype(BF16)
    upper = jnp.where(lax.broadcasted_iota(I32, (page, page), 0) <= lax.broadcasted_iota(I32, (page, page), 1),
                      1.0, 0.0).astype(BF16)
    rank = run_s[...] + jnp.dot(eqb, upper, preferred_element_type=F32)[0:1]
    run_s[...] = rank[:, page - 1:page]
    sel = (key > thr) | (eq & (rank <= need))
    kb = k_ref[...].astype(BF16)
    vb = v_ref[...].astype(BF16)
    for h in range(N_HEADS):
        hs = slice(h * HEAD_DIM, (h + 1) * HEAD_DIM)
        qh = _pad_rows(rowv[:, C_DQ + h * HEAD_DIM:C_DQ + (h + 1) * HEAD_DIM], 8).astype(BF16)
        s = _nt_dot(qh, kb[:, hs])[0:1] * (HEAD_DIM ** -0.5)
        update(h, s, sel, lambda pr: jnp.dot(_pad_rows(pr, 8).astype(BF16), vb[:, hs],
                                             preferred_element_type=F32)[0:1])

    @pl.when(p == n_pages - 1)
    def _():
        key_n = keys_ref[pl.ds(n_pages, 1), 0:1]
        eq_n = key_n == thr
        sel_n = (key_n > thr) | (eq_n & (run_s[...] + 1.0 <= need))
        for h in range(N_HEADS):
            hs = lambda base: slice(base + h * HEAD_DIM, base + (h + 1) * HEAD_DIM)
            qh = rowv[:, hs(C_DQ)]
            kn = rowv[:, hs(C_DK)]
            vn = rowv[:, hs(C_DV)]
            s = jnp.sum(qh * kn, axis=1, keepdims=True) * (HEAD_DIM ** -0.5)
            update(h, s, sel_n, lambda pr: pr * vn)
            o = acc_s[h] / l_s[h]
            y_ref[:, h * HEAD_DIM:(h + 1) * HEAD_DIM] = (o * _silu(rowv[:, hs(C_DZ)])).astype(BF16)


def _sample_dsa(page_table, proj, cache_k, cache_v, layer, keys, thr, need):
    nb, n_pages = page_table.shape
    page = cache_k.shape[2]
    pblk = pl.BlockSpec((None, None, page, GROUP_W), lambda b, p, pt: (layer, pt[b, p], 0, 0))
    per_b = lambda r, c: pl.BlockSpec((None, r, c), lambda b, p, pt: (b, 0, 0))
    grid_spec = pltpu.PrefetchScalarGridSpec(
        num_scalar_prefetch=1,
        grid=(nb, n_pages),
        in_specs=[pl.BlockSpec(proj.shape, lambda b, p, pt: (0, 0)), pblk, pblk,
                  per_b(keys.shape[1], page), per_b(1, LANES), per_b(1, LANES)],
        out_specs=per_b(1, GROUP_W),
        scratch_shapes=[pltpu.VMEM((1, 1), F32), pltpu.VMEM((N_HEADS, 1, 1), F32),
                        pltpu.VMEM((N_HEADS, 1, 1), F32), pltpu.VMEM((N_HEADS, 1, HEAD_DIM), F32)],
    )
    return pl.pallas_call(
        functools.partial(_sample_dsa_kernel, n_pages=n_pages),
        grid_spec=grid_spec,
        out_shape=jax.ShapeDtypeStruct((nb, 1, GROUP_W), BF16),
        compiler_params=_cparams(("parallel", "arbitrary")),
        name="sample_dsa",
    )(page_table, proj, cache_k, cache_v, keys, thr, need)


def _reorder_w_in(w_in):
    depth, d, _ = w_in.shape
    pad = jnp.zeros((depth, d, PROJ_PAD - PROJ_USED), w_in.dtype)
    w = jnp.concatenate([w_in[..., :ORIG_GATES], w_in[..., ORIG_GATES + 2 * N_HEADS:],
                         w_in[..., ORIG_GATES:ORIG_GATES + 2 * N_HEADS], pad], axis=-1)
    return w.astype(BF16)


def _prompt_layer(x, layer, lw, final, g_final):
    t = x.shape[0]
    proj = _inproj(x, lw["g_norm"], lw["w_in"], tm=min(1024, t))
    gates = proj[:, C_TAIL + T_AI:C_TAIL + T_AI + 2 * N_HEADS]
    lam_init = 0.8 - 0.6 * math.exp(-0.3 * layer)
    L = min(256, t)
    ya, c1, n1, m1 = _mlstm_prompt(proj, gates, gates.T, lw["b_igate"], lw["b_fgate"], lw["g_mlstm"], L)
    yb = _gmlp_prompt(proj, lw["g_gmlp"], lw["w_spatial"], lw["b_spatial"].T, tr=min(512, t))
    yc = _diff_prompt(proj, lw["lam_p"], lw["g_diff"], lam_init, tq=min(512, t), tk=min(512, t))
    kidx = proj[:, C_TAIL:C_TAIL + IDX_DIM]
    kidx_t4 = jnp.tile(kidx.T.astype(BF16), (N_IDX_HEADS, 1))
    yd = _dsa_prompt(proj, kidx_t4, tq=min(256, t), rg=64)
    x_new = _outproj(x, (ya, yb, yc, yd), lw["w_out"], g_final, tm=min(256, t), final=final)
    heads = lambda c0: proj[:, c0:c0 + GROUP_W].reshape(1, t, N_HEADS, HEAD_DIM)
    new = {"mlstm_c": c1[None], "mlstm_n": n1[:, 0][None], "mlstm_m": m1[:, 0, 0][None],
           "diff_k": heads(C_CK), "diff_v": heads(C_CV), "dsa_k": heads(C_DK), "dsa_v": heads(C_DV),
           "dsa_kidx": kidx[None]}
    return x_new, new


def _sample_layer(x, layer, lw, past, page_table, caches, final, g_final):
    nb = x.shape[0]
    proj = _inproj(x, lw["g_norm"], lw["w_in"], tm=nb)
    lam_init = 0.8 - 0.6 * math.exp(-0.3 * layer)
    ya, yb, gv, c1, n1, m1 = _sample_small(
        proj, past["c"], past["n"][:, :, None, :], jnp.broadcast_to(past["m"][:, :, None, None], (nb, N_HEADS, 1, LANES)),
        lw["b_igate"], lw["b_fgate"], lw["g_mlstm"], lw["g_gmlp"],
        lw["w_spatial"][:, 0, 0], lw["b_spatial"][:, 0])
    yc = _sample_diff(page_table, proj, caches["diff_k"], caches["diff_v"], layer, lw["lam_p"], lw["g_diff"], lam_init)
    n_pages = page_table.shape[1]
    page = caches["dsa_k"].shape[2]
    topk = min(TOPK_MAX, (n_pages * page + 1) // 4)
    keys, thr, need = _sample_scores(page_table, proj, caches["dsa_kidx"], layer, topk)
    yd = _sample_dsa(page_table, proj, caches["dsa_k"], caches["dsa_v"], layer, keys, thr, need)
    flat = lambda y: y.reshape(nb, GROUP_W)
    x_new = _outproj(x, (flat(ya), flat(yb), flat(yc), flat(yd)), lw["w_out"], g_final, tm=nb, final=final)
    heads = lambda c0: proj[:, c0:c0 + GROUP_W].reshape(nb, 1, N_HEADS, HEAD_DIM)
    new = {"mlstm_c": c1, "mlstm_n": n1[:, :, 0], "mlstm_m": m1[:, :, 0, 0],
           "diff_k": heads(C_CK), "diff_v": heads(C_CV), "dsa_k": heads(C_DK), "dsa_v": heads(C_DV),
           "dsa_kidx": proj[:, C_TAIL:C_TAIL + IDX_DIM].reshape(nb, 1, IDX_DIM), "gmlp_v": gv}
    return x_new, new


def kernel(x_prompt, x_sample, state_mlstm_c, state_mlstm_n, state_mlstm_m, cache_diff_k, cache_diff_v, cache_dsa_k, cache_dsa_v, cache_dsa_kidx, page_table, g_norm, w_in, b_igate, b_fgate, g_mlstm, g_gmlp, w_spatial, b_spatial, lambda_q1, lambda_k1, lambda_q2, lambda_k2, g_diff, w_out, g_final):
    depth = w_in.shape[0]
    assert x_prompt.shape[0] == 1 and x_sample.shape[1] == 1
    assert w_in.shape[2] == PROJ_USED
    xp = x_prompt[0]
    xs = x_sample[:, 0]
    w_in_r = _reorder_w_in(w_in)
    w_out_b = w_out.astype(BF16)
    n_phys, page = cache_diff_k.shape[1], cache_diff_k.shape[2]
    paged = lambda c: c.reshape(depth, n_phys, page, GROUP_W)
    caches = {"diff_k": paged(cache_diff_k), "diff_v": paged(cache_diff_v),
              "dsa_k": paged(cache_dsa_k), "dsa_v": paged(cache_dsa_v), "dsa_kidx": cache_dsa_kidx}
    g_final2 = g_final[None]
    pn, sn = [], []
    for l in range(depth):
        lw = {"g_norm": g_norm[l][None], "w_in": w_in_r[l], "b_igate": b_igate[l], "b_fgate": b_fgate[l],
              "g_mlstm": g_mlstm[l][None], "g_gmlp": g_gmlp[l][None], "w_spatial": w_spatial[l],
              "b_spatial": b_spatial[l], "g_diff": g_diff[l][None], "w_out": w_out_b[l],
              "lam_p": jnp.stack([lambda_q1[l], lambda_k1[l], lambda_q2[l], lambda_k2[l]])}
        final = l == depth - 1
        xp, p_new = _prompt_layer(xp, l, lw, final, g_final2)
        past = {"c": state_mlstm_c[l], "n": state_mlstm_n[l], "m": state_mlstm_m[l]}
        xs, s_new = _sample_layer(xs, l, lw, past, page_table, caches, final, g_final2)
        pn.append(p_new)
        sn.append(s_new)
    st = lambda lst, name: jnp.stack([d[name] for d in lst])
    names = ("mlstm_c", "mlstm_n", "mlstm_m", "diff_k", "diff_v", "dsa_k", "dsa_v", "dsa_kidx")
    return ((xp[None], xs[:, None]) + tuple(st(pn, n) for n in names) + tuple(st(sn, n) for n in names)
            + (st(sn, "gmlp_v"),))
```

```python
import functools
import math

import jax
import jax.numpy as jnp
from jax import lax
from jax.experimental import pallas as pl
from jax.experimental.pallas import tpu as pltpu

F32 = jnp.float32
BF16 = jnp.bfloat16
I32 = jnp.int32

N_HEADS = 4
HEAD_DIM = 128
GROUP_W = N_HEADS * HEAD_DIM
DIFF_DIM = HEAD_DIM // 2
N_IDX_HEADS = 4
IDX_DIM = 64
TOPK_MAX = 256
EPS = 1e-6
LANES = 128
NEG_BIG = -1e30
INT_MIN = -(2 ** 31)

C_AQ, C_AK, C_AV, C_AO, C_AZ = 0, 512, 1024, 1536, 2048
C_BU, C_BV, C_BZ = 2560, 3072, 3584
C_CQ, C_CK, C_CV, C_CZ = 4096, 4608, 5120, 5632
C_DQ, C_DK, C_DV, C_DZ = 6144, 6656, 7168, 7680
C_IQ = 8192
C_TAIL = 8448
T_IW, T_AI, T_AF = 64, 68, 72
PROJ_USED = 8524
PROJ_PAD = 8704
ORIG_GATES = 2560

VMEM_LIMIT = 48 * 1024 * 1024
SAMPLE_PAGE_GROUP = 8


def _cparams(sem):
    return pltpu.CompilerParams(dimension_semantics=sem, vmem_limit_bytes=VMEM_LIMIT)


def _silu(z):
    return z * jax.nn.sigmoid(z)


def _log_sigmoid(x):
    return jnp.minimum(x, 0.0) - jnp.log1p(jnp.exp(-jnp.abs(x)))


def _nt_dot(a, b):
    return lax.dot_general(a, b, (((1,), (1,)), ((), ())), preferred_element_type=F32)


def _tn_dot(a, b):
    return lax.dot_general(a, b, (((0,), (0,)), ((), ())), preferred_element_type=F32)


def _sort_key(x):
    bits = lax.bitcast_convert_type(x + 0.0, I32)
    return bits ^ ((bits >> 31) & 0x7FFFFFFF)


def _inproj_kernel(x_ref, g_ref, w_ref, o_ref, xn_ref):
    @pl.when(pl.program_id(1) == 0)
    def _():
        x = x_ref[...]
        ms = jnp.mean(x * x, axis=-1, keepdims=True)
        xn_ref[...] = (x * lax.rsqrt(ms + EPS) * g_ref[...]).astype(BF16)

    o_ref[...] = jnp.dot(xn_ref[...], w_ref[...], preferred_element_type=F32)


def _inproj(x, g, w, tm):
    m, d = x.shape
    tn = 512
    return pl.pallas_call(
        _inproj_kernel,
        grid=(m // tm, PROJ_PAD // tn),
        in_specs=[pl.BlockSpec((tm, d), lambda i, j: (i, 0)),
                  pl.BlockSpec((1, d), lambda i, j: (0, 0)),
                  pl.BlockSpec((d, tn), lambda i, j: (0, j))],
        out_specs=pl.BlockSpec((tm, tn), lambda i, j: (i, j)),
        out_shape=jax.ShapeDtypeStruct((m, PROJ_PAD), F32),
        scratch_shapes=[pltpu.VMEM((tm, d), BF16)],
        compiler_params=_cparams(("parallel", "arbitrary")),
        name="inproj",
    )(x, g, w)


def _outproj_kernel(x_ref, a_ref, b_ref, c_ref, d_ref, w_ref, gf_ref, o_ref, *, final):
    acc = x_ref[...]
    for i, m_ref in enumerate((a_ref, b_ref, c_ref, d_ref)):
        acc = acc + jnp.dot(m_ref[...], w_ref[i * GROUP_W:(i + 1) * GROUP_W, :], preferred_element_type=F32)
    if final:
        ms = jnp.mean(acc * acc, axis=-1, keepdims=True)
        acc = acc * lax.rsqrt(ms + EPS) * gf_ref[...]
    o_ref[...] = acc


def _outproj(x, mixes, w, gf, tm, final):
    m, d = x.shape
    row = lambda i: (i, 0)
    const = lambda i: (0, 0)
    return pl.pallas_call(
        functools.partial(_outproj_kernel, final=final),
        grid=(m // tm,),
        in_specs=[pl.BlockSpec((tm, d), row)] + [pl.BlockSpec((tm, GROUP_W), row)] * 4
                 + [pl.BlockSpec((4 * GROUP_W, d), const), pl.BlockSpec((1, d), const)],
        out_specs=pl.BlockSpec((tm, d), row),
        out_shape=jax.ShapeDtypeStruct((m, d), F32),
        compiler_params=_cparams(("parallel",)),
        name="outproj",
    )(x, *mixes, w, gf)


def _mlstm_kernel(q_ref, k_ref, v_ref, o_ref, z_ref, g_ref, gt_ref, bi_ref, bf_ref, gm_ref,
                  y_ref, c_ref, n_ref, m_ref, *, L):
    h = pl.program_id(0)

    @pl.when(pl.program_id(1) == 0)
    def _():
        c_ref[...] = jnp.zeros_like(c_ref)
        n_ref[...] = jnp.zeros_like(n_ref)
        m_ref[...] = jnp.zeros_like(m_ref)

    bi = bi_ref[h]
    bf = bf_ref[h]
    q = q_ref[...] * (HEAD_DIM ** -0.5)
    k = k_ref[...]
    qb = q.astype(BF16)
    kb = k.astype(BF16)
    vb = v_ref[...].astype(BF16)

    li_r = gt_ref[pl.ds(h, 1), :] + bi
    lf_r = _log_sigmoid(gt_ref[pl.ds(h + N_HEADS, 1), :] + bf)
    g = g_ref[...]
    lane8 = lax.broadcasted_iota(I32, g.shape, 1)
    li_c = jnp.sum(jnp.where(lane8 == h, g, 0.0), axis=1, keepdims=True) + bi
    lf_c = _log_sigmoid(jnp.sum(jnp.where(lane8 == h + N_HEADS, g, 0.0), axis=1, keepdims=True) + bf)

    row = lax.broadcasted_iota(I32, (L, L), 0)
    col = lax.broadcasted_iota(I32, (L, L), 1)
    causal = col <= row
    b_c = jnp.sum(jnp.where(causal, lf_r, 0.0), axis=1, keepdims=True)
    b_r = jnp.sum(jnp.where(row <= col, lf_c, 0.0), axis=0, keepdims=True)
    b_last = jnp.sum(lf_r, axis=1, keepdims=True)

    m0 = m_ref[:, 0:1]
    c0 = c_ref[...]
    n0 = n_ref[...]
    dmat = jnp.where(causal, b_c - b_r + li_r, -jnp.inf)
    inter = b_c + m0
    m_t = jnp.maximum(inter, jnp.max(dmat, axis=1, keepdims=True))
    w_inter = jnp.exp(inter - m_t)
    s = _nt_dot(qb, kb) * jnp.exp(dmat - m_t)
    num = (w_inter * jnp.dot(qb, c0.astype(BF16), preferred_element_type=F32)
           + jnp.dot(s.astype(BF16), vb, preferred_element_type=F32))
    nq = w_inter * jnp.sum(q * n0, axis=1, keepdims=True) + jnp.sum(s, axis=1, keepdims=True)
    den = jnp.maximum(jnp.abs(nq), jnp.exp(-m_t))
    hh = num / den

    g_r = b_last - b_r + li_r
    g_c = b_last - b_c + li_c
    m_new = jnp.maximum(b_last + m0, jnp.max(g_r, axis=1, keepdims=True))
    ws_c = jnp.exp(g_c - m_new)
    wc = jnp.exp(b_last + m0 - m_new)
    kw = k * ws_c
    c_ref[...] = wc * c0 + _tn_dot(kw.astype(BF16), vb)
    n_ref[...] = wc * n0 + jnp.sum(kw, axis=0, keepdims=True)
    m_ref[...] = jnp.broadcast_to(m_new, m_ref.shape)

    ha = jax.nn.sigmoid(o_ref[...]) * hh
    y = ha * lax.rsqrt(jnp.mean(ha * ha, axis=1, keepdims=True) + EPS) * gm_ref[...]
    y_ref[...] = (y * _silu(z_ref[...])).astype(BF16)


def _mlstm_prompt(proj, gates, gates_t, b_i, b_f, g_mlstm, L):
    t = proj.shape[0]
    hb = lambda base: (lambda h, c: (c, base // HEAD_DIM + h))
    smem = pl.BlockSpec(memory_space=pltpu.SMEM)
    return pl.pallas_call(
        functools.partial(_mlstm_kernel, L=L),
        grid=(N_HEADS, t // L),
        in_specs=[pl.BlockSpec((L, HEAD_DIM), hb(C_AQ)), pl.BlockSpec((L, HEAD_DIM), hb(C_AK)),
                  pl.BlockSpec((L, HEAD_DIM), hb(C_AV)), pl.BlockSpec((L, HEAD_DIM), hb(C_AO)),
                  pl.BlockSpec((L, HEAD_DIM), hb(C_AZ)),
                  pl.BlockSpec((L, 2 * N_HEADS), lambda h, c: (c, 0)),
                  pl.BlockSpec((2 * N_HEADS, L), lambda h, c: (0, c)),
                  smem, smem,
                  pl.BlockSpec((1, HEAD_DIM), lambda h, c: (0, h))],
        out_specs=[pl.BlockSpec((L, HEAD_DIM), lambda h, c: (c, h)),
                   pl.BlockSpec((None, HEAD_DIM, HEAD_DIM), lambda h, c: (h, 0, 0)),
                   pl.BlockSpec((None, 1, HEAD_DIM), lambda h, c: (h, 0, 0)),
                   pl.BlockSpec((None, 1, LANES), lambda h, c: (h, 0, 0))],
        out_shape=[jax.ShapeDtypeStruct((t, GROUP_W), BF16),
                   jax.ShapeDtypeStruct((N_HEADS, HEAD_DIM, HEAD_DIM), F32),
                   jax.ShapeDtypeStruct((N_HEADS, 1, HEAD_DIM), F32),
                   jax.ShapeDtypeStruct((N_HEADS, 1, LANES), F32)],
        compiler_params=_cparams(("parallel", "arbitrary")),
        name="mlstm_prompt",
    )(proj, proj, proj, proj, proj, gates, gates_t, b_i, b_f, g_mlstm)


def _gmlp_kernel(u_ref, v_ref, z_ref, gg_ref, w_ref, bs_ref, y_ref, *, C):
    u = jax.nn.gelu(u_ref[...])
    gv = jax.nn.gelu(v_ref[...])
    v = gv * lax.rsqrt(jnp.mean(gv * gv, axis=1, keepdims=True) + EPS) * gg_ref[...]
    sz = _silu(z_ref[...])
    row = lax.broadcasted_iota(I32, (C, C), 0)
    col = lax.broadcasted_iota(I32, (C, C), 1)
    for g in range(N_HEADS):
        w = jnp.where(col <= row, w_ref[g], 0.0).astype(BF16)
        bcol = bs_ref[:, g:g + 1]
        gs = slice(g * HEAD_DIM, (g + 1) * HEAD_DIM)
        for r in range(u.shape[0] // C):
            rs = slice(r * C, (r + 1) * C)
            mixed = jnp.dot(w, v[rs, gs].astype(BF16), preferred_element_type=F32) + bcol
            y_ref[rs, gs] = (u[rs, gs] * mixed * sz[rs, gs]).astype(BF16)


def _gmlp_prompt(proj, g_gmlp, w_spatial, b_spatial_t, tr):
    t = proj.shape[0]
    c = w_spatial.shape[-1]
    blk = lambda base: pl.BlockSpec((tr, GROUP_W), lambda i: (i, base // GROUP_W))
    return pl.pallas_call(
        functools.partial(_gmlp_kernel, C=c),
        grid=(t // tr,),
        in_specs=[blk(C_BU), blk(C_BV), blk(C_BZ),
                  pl.BlockSpec((1, GROUP_W), lambda i: (0, 0)),
                  pl.BlockSpec((N_HEADS, c, c), lambda i: (0, 0, 0)),
                  pl.BlockSpec((c, N_HEADS), lambda i: (0, 0))],
        out_specs=pl.BlockSpec((tr, GROUP_W), lambda i: (i, 0)),
        out_shape=jax.ShapeDtypeStruct((t, GROUP_W), BF16),
        compiler_params=_cparams(("parallel",)),
        name="gmlp_prompt",
    )(proj, proj, proj, g_gmlp, w_spatial, b_spatial_t)


def _lambda(lam_ref, lam_init):
    lp = lam_ref[...]
    s1 = jnp.sum(lp[0:1] * lp[1:2], axis=1, keepdims=True)
    s2 = jnp.sum(lp[2:3] * lp[3:4], axis=1, keepdims=True)
    return jnp.exp(s1) - jnp.exp(s2) + lam_init


def _diff_finish(o, gd, z, lam_init):
    y = o * lax.rsqrt(jnp.mean(o * o, axis=1, keepdims=True) + EPS) * gd
    return (y * (1.0 - lam_init)) * _silu(z)


def _diff_kernel(q_ref, k_ref, v_ref, z_ref, lam_ref, gd_ref, y_ref,
                 q0_s, q1_s, m_s, l_s, acc_s, *, lam_init, tq, tk):
    qi = pl.program_id(1)
    ki = pl.program_id(2)

    @pl.when(ki == 0)
    def _():
        q = q_ref[...]
        lane = lax.broadcasted_iota(I32, q.shape, 1)
        q0_s[...] = jnp.where(lane < DIFF_DIM, q, 0.0).astype(BF16)
        q1_s[...] = jnp.where(lane >= DIFF_DIM, q, 0.0).astype(BF16)
        m_s[...] = jnp.full_like(m_s, NEG_BIG)
        l_s[...] = jnp.zeros_like(l_s)
        acc_s[...] = jnp.zeros_like(acc_s)

    @pl.when(ki <= qi)
    def _():
        kb = k_ref[...].astype(BF16)
        vb = v_ref[...].astype(BF16)
        row = qi * tq + lax.broadcasted_iota(I32, (tq, tk), 0)
        col = ki * tk + lax.broadcasted_iota(I32, (tq, tk), 1)
        msk = col <= row
        for j, qs in enumerate((q0_s, q1_s)):
            s = jnp.where(msk, _nt_dot(qs[...], kb) * (DIFF_DIM ** -0.5), NEG_BIG)
            m_old = m_s[j]
            m_new = jnp.maximum(m_old, jnp.max(s, axis=1, keepdims=True))
            p = jnp.where(msk, jnp.exp(s - m_new), 0.0)
            alpha = jnp.exp(m_old - m_new)
            l_s[j] = alpha * l_s[j] + jnp.sum(p, axis=1, keepdims=True)
            acc_s[j] = alpha * acc_s[j] + jnp.dot(p.astype(BF16), vb, preferred_element_type=F32)
            m_s[j] = m_new

    @pl.when(ki == pl.num_programs(2) - 1)
    def _():
        lam = _lambda(lam_ref, lam_init)
        o = acc_s[0] / l_s[0] - lam * (acc_s[1] / l_s[1])
        y_ref[...] = _diff_finish(o, gd_ref[...], z_ref[...], lam_init).astype(BF16)


def _diff_prompt(proj, lam_p, g_diff, lam_init, tq, tk):
    t = proj.shape[0]
    nq, nk = t // tq, t // tk
    kv = lambda base: pl.BlockSpec((tk, HEAD_DIM), lambda h, i, j: (jnp.minimum(j, i), base // HEAD_DIM + h))
    qz = lambda base: pl.BlockSpec((tq, HEAD_DIM), lambda h, i, j: (i, base // HEAD_DIM + h))
    return pl.pallas_call(
        functools.partial(_diff_kernel, lam_init=lam_init, tq=tq, tk=tk),
        grid=(N_HEADS, nq, nk),
        in_specs=[qz(C_CQ), kv(C_CK), kv(C_CV), qz(C_CZ),
                  pl.BlockSpec((4, DIFF_DIM), lambda h, i, j: (0, 0)),
                  pl.BlockSpec((1, HEAD_DIM), lambda h, i, j: (0, 0))],
        out_specs=pl.BlockSpec((tq, HEAD_DIM), lambda h, i, j: (i, h)),
        out_shape=jax.ShapeDtypeStruct((t, GROUP_W), BF16),
        scratch_shapes=[pltpu.VMEM((tq, HEAD_DIM), BF16), pltpu.VMEM((tq, HEAD_DIM), BF16),
                        pltpu.VMEM((2, tq, 1), F32), pltpu.VMEM((2, tq, 1), F32),
                        pltpu.VMEM((2, tq, HEAD_DIM), F32)],
        compiler_params=_cparams(("parallel", "parallel", "arbitrary")),
        name="diff_prompt",
    )(proj, proj, proj, proj, lam_p, g_diff)


def _kth_largest(count_ge, shape, kk):
    thr = jnp.where(count_ge(jnp.zeros(shape, I32)) >= kk, 0, INT_MIN).astype(I32)

    def bit_step(i, thr):
        cand = thr | (jnp.int32(1) << (30 - i))
        return jnp.where(count_ge(cand) >= kk, cand, thr)

    return lax.fori_loop(0, 31, bit_step, thr)


def _dsa_kernel(q_ref, k_ref, v_ref, z_ref, iq_ref, tail_ref, kt_ref, y_ref,
                keys_s, thr_s, need_s, run_s, m_s, l_s, acc_s, *, tq, topk, rg):
    qi = pl.program_id(0)
    ki = pl.program_id(1)
    tk = tq

    @pl.when(ki == 0)
    def _():
        iq = iq_ref[...]
        grp = lax.broadcasted_iota(I32, iq.shape, 1) // IDX_DIM
        qh = [jnp.where(grp == h, iq, 0.0).astype(BF16) for h in range(N_IDX_HEADS)]
        tail = tail_ref[...]
        wh = [tail[:, T_IW + h:T_IW + h + 1] * (N_IDX_HEADS ** -0.5) for h in range(N_IDX_HEADS)]
        rowg = qi * tq + lax.broadcasted_iota(I32, (tq, tk), 0)
        col0 = lax.broadcasted_iota(I32, (tq, tk), 1)

        def score_chunk(c, carry):
            off = pl.multiple_of(c * tk, tk)
            kt = kt_ref[:, pl.ds(off, tk)]
            sc = jnp.zeros((tq, tk), F32)
            for h in range(N_IDX_HEADS):
                sh = jnp.dot(qh[h], kt, preferred_element_type=F32) * (IDX_DIM ** -0.5)
                sc = sc + wh[h] * jnp.maximum(sh, 0.0)
            keys_s[:, pl.ds(off, tk)] = jnp.where(c * tk + col0 <= rowg, _sort_key(sc), INT_MIN)
            return carry

        lax.fori_loop(0, qi + 1, score_chunk, 0)

        def search_rows(g, carry):
            r0 = pl.multiple_of(g * rg, rg)

            def count(pred):
                def body(c, acc):
                    kk = keys_s[pl.ds(r0, rg), pl.ds(pl.multiple_of(c * tk, tk), tk)]
                    hit = jnp.where(pred(kk), 1, 0).astype(I32)
                    for j in range(tk // LANES):
                        acc = acc + hit[:, j * LANES:(j + 1) * LANES]
                    return acc
                acc = lax.fori_loop(0, qi + 1, body, jnp.zeros((rg, LANES), I32))
                return jnp.sum(acc, axis=1, keepdims=True)

            thr = _kth_largest(lambda cand: count(lambda kk: kk >= cand), (rg, 1), topk)
            thr_s[pl.ds(r0, rg), :] = thr
            need_s[pl.ds(r0, rg), :] = (topk - count(lambda kk: kk > thr)).astype(F32)
            return carry

        lax.fori_loop(0, tq // rg, search_rows, 0)
        run_s[...] = jnp.zeros_like(run_s)
        m_s[...] = jnp.full_like(m_s, NEG_BIG)
        l_s[...] = jnp.zeros_like(l_s)
        acc_s[...] = jnp.zeros_like(acc_s)

    @pl.when(ki <= qi)
    def _():
        key = keys_s[:, pl.ds(pl.multiple_of(ki * tk, tk), tk)]
        thr = thr_s[...]
        eq = key == thr
        eqb = jnp.where(eq, 1.0, 0.0).astype(BF16)
        upper = jnp.where(lax.broadcasted_iota(I32, (tk, tk), 0) <= lax.broadcasted_iota(I32, (tk, tk), 1),
                          1.0, 0.0).astype(BF16)
        rank = run_s[...] + jnp.dot(eqb, upper, preferred_element_type=F32)
        run_s[...] = rank[:, tk - 1:tk]
        row = qi * tq + lax.broadcasted_iota(I32, (tq, tk), 0)
        col = ki * tk + lax.broadcasted_iota(I32, (tq, tk), 1)
        sel = ((key > thr) | (eq & (rank <= need_s[...]))) & (col <= row)
        q = q_ref[...].astype(BF16)
        kb = k_ref[...].astype(BF16)
        vb = v_ref[...].astype(BF16)
        for h in range(N_HEADS):
            hs = slice(h * HEAD_DIM, (h + 1) * HEAD_DIM)
            s = jnp.where(sel, _nt_dot(q[:, hs], kb[:, hs]) * (HEAD_DIM ** -0.5), NEG_BIG)
            m_old = m_s[h]
            m_new = jnp.maximum(m_old, jnp.max(s, axis=1, keepdims=True))
            p = jnp.where(sel, jnp.exp(s - m_new), 0.0)
            alpha = jnp.exp(m_old - m_new)
            l_s[h] = alpha * l_s[h] + jnp.sum(p, axis=1, keepdims=True)
            acc_s[:, hs] = alpha * acc_s[:, hs] + jnp.dot(p.astype(BF16), vb[:, hs], preferred_element_type=F32)
            m_s[h] = m_new

    @pl.when(ki == pl.num_programs(1) - 1)
    def _():
        sz = _silu(z_ref[...])
        for h in range(N_HEADS):
            hs = slice(h * HEAD_DIM, (h + 1) * HEAD_DIM)
            y_ref[:, hs] = ((acc_s[:, hs] / l_s[h]) * sz[:, hs]).astype(BF16)


def _dsa_prompt(proj, kidx_t4, tq, rg):
    t = proj.shape[0]
    n = t // tq
    topk = min(TOPK_MAX, t // 4)
    qblk = lambda base, w: pl.BlockSpec((tq, w), lambda i, j: (i, base // w))
    kblk = lambda base: pl.BlockSpec((tq, GROUP_W), lambda i, j: (jnp.minimum(j, i), base // GROUP_W))
    return pl.pallas_call(
        functools.partial(_dsa_kernel, tq=tq, topk=topk, rg=rg),
        grid=(n, n),
        in_specs=[qblk(C_DQ, GROUP_W), kblk(C_DK), kblk(C_DV), qblk(C_DZ, GROUP_W),
                  qblk(C_IQ, N_IDX_HEADS * IDX_DIM), qblk(C_TAIL, LANES),
                  pl.BlockSpec((N_IDX_HEADS * IDX_DIM, t), lambda i, j: (0, 0))],
        out_specs=pl.BlockSpec((tq, GROUP_W), lambda i, j: (i, 0)),
        out_shape=jax.ShapeDtypeStruct((t, GROUP_W), BF16),
        scratch_shapes=[pltpu.VMEM((tq, t), I32), pltpu.VMEM((tq, 1), I32), pltpu.VMEM((tq, 1), F32),
                        pltpu.VMEM((tq, 1), F32), pltpu.VMEM((N_HEADS, tq, 1), F32),
                        pltpu.VMEM((N_HEADS, tq, 1), F32), pltpu.VMEM((tq, GROUP_W), F32)],
        compiler_params=_cparams(("parallel", "arbitrary")),
        name="dsa_prompt",
    )(proj, proj, proj, proj, proj, proj, kidx_t4)


def _sample_small_kernel(proj_ref, c0_ref, n0_ref, m0_ref, bi_ref, bf_ref, gm_ref, gg_ref, w00_ref, b0_ref,
                         ya_ref, yb_ref, gv_ref, c1_ref, n1_ref, m1_ref):
    b = pl.program_id(0)
    rowv = proj_ref[pl.ds(b, 1), :]
    eye = lax.broadcasted_iota(I32, (HEAD_DIM, HEAD_DIM), 0) == lax.broadcasted_iota(I32, (HEAD_DIM, HEAD_DIM), 1)
    for h in range(N_HEADS):
        hs = lambda base: slice(base + h * HEAD_DIM, base + (h + 1) * HEAD_DIM)
        q = rowv[:, hs(C_AQ)] * (HEAD_DIM ** -0.5)
        k = rowv[:, hs(C_AK)]
        v = rowv[:, hs(C_AV)]
        li = rowv[:, C_TAIL + T_AI + h:C_TAIL + T_AI + h + 1] + bi_ref[h]
        lf = _log_sigmoid(rowv[:, C_TAIL + T_AF + h:C_TAIL + T_AF + h + 1] + bf_ref[h])
        c0 = c0_ref[h]
        n0 = n0_ref[h]
        m0 = m0_ref[h][:, 0:1]
        inter = lf + m0
        m_t = jnp.maximum(inter, li)
        w_inter = jnp.exp(inter - m_t)
        w_new = jnp.exp(li - m_t)
        s = jnp.sum(q * k, axis=1, keepdims=True) * w_new
        qc = jnp.dot(q.astype(BF16), c0.astype(BF16), preferred_element_type=F32)
        num = w_inter * qc + s * v
        nq = w_inter * jnp.sum(q * n0, axis=1, keepdims=True) + s
        den = jnp.maximum(jnp.abs(nq), jnp.exp(-m_t))
        hh = num / den
        k_col = jnp.sum(jnp.where(eye, k, 0.0), axis=1, keepdims=True)
        c1_ref[h] = w_inter * c0 + w_new * (k_col * v)
        n1_ref[h] = w_inter * n0 + w_new * k
        m1_ref[h] = jnp.broadcast_to(m_t, (1, LANES))
        ha = jax.nn.sigmoid(rowv[:, hs(C_AO)]) * hh
        y = ha * lax.rsqrt(jnp.mean(ha * ha, axis=1, keepdims=True) + EPS) * gm_ref[:, h * HEAD_DIM:(h + 1) * HEAD_DIM]
        ya_ref[:, h * HEAD_DIM:(h + 1) * HEAD_DIM] = (y * _silu(rowv[:, hs(C_AZ)])).astype(BF16)

    u = jax.nn.gelu(rowv[:, C_BU:C_BU + GROUP_W])
    gv = jax.nn.gelu(rowv[:, C_BV:C_BV + GROUP_W])
    vv = gv * lax.rsqrt(jnp.mean(gv * gv, axis=1, keepdims=True) + EPS) * gg_ref[...]
    gv_ref[...] = vv
    sz = _silu(rowv[:, C_BZ:C_BZ + GROUP_W])
    for g in range(N_HEADS):
        gs = slice(g * HEAD_DIM, (g + 1) * HEAD_DIM)
        mixed = w00_ref[g] * vv[:, gs] + b0_ref[g]
        yb_ref[:, gs] = (u[:, gs] * mixed * sz[:, gs]).astype(BF16)


def _sample_small(proj, c0, n0, m0, b_i, b_f, g_mlstm, g_gmlp, w00, b0):
    nb = proj.shape[0]
    smem = pl.BlockSpec(memory_space=pltpu.SMEM)
    per_b = lambda *tail: pl.BlockSpec((None,) + tail, lambda b: (b,) + (0,) * len(tail))
    row_out = lambda dt: jax.ShapeDtypeStruct((nb, 1, GROUP_W), dt)
    return pl.pallas_call(
        _sample_small_kernel,
        grid=(nb,),
        in_specs=[pl.BlockSpec(proj.shape, lambda b: (0, 0)),
                  per_b(N_HEADS, HEAD_DIM, HEAD_DIM), per_b(N_HEADS, 1, HEAD_DIM), per_b(N_HEADS, 1, LANES),
                  smem, smem,
                  pl.BlockSpec((1, GROUP_W), lambda b: (0, 0)), pl.BlockSpec((1, GROUP_W), lambda b: (0, 0)),
                  smem, smem],
        out_specs=[per_b(1, GROUP_W), per_b(1, GROUP_W), per_b(1, GROUP_W),
                   per_b(N_HEADS, HEAD_DIM, HEAD_DIM), per_b(N_HEADS, 1, HEAD_DIM), per_b(N_HEADS, 1, LANES)],
        out_shape=[row_out(BF16), row_out(BF16), row_out(F32),
                   jax.ShapeDtypeStruct((nb, N_HEADS, HEAD_DIM, HEAD_DIM), F32),
                   jax.ShapeDtypeStruct((nb, N_HEADS, 1, HEAD_DIM), F32),
                   jax.ShapeDtypeStruct((nb, N_HEADS, 1, LANES), F32)],
        compiler_params=_cparams(("parallel",)),
        name="sample_small",
    )(proj, c0, n0, m0, b_i, b_f, g_mlstm, g_gmlp, w00, b0)


def _pad_rows(x, n):
    return jnp.concatenate([x, jnp.zeros((n - x.shape[0], x.shape[1]), x.dtype)], axis=0)


def _paged_specs(cache, layer, group):
    rows, lanes = cache.shape[2], cache.shape[3]
    return [pl.BlockSpec((None, None, rows, lanes), functools.partial(
        lambda b, p, pt, g: (layer, pt[b, p * group + g], 0, 0), g=g)) for g in range(group)]


def _head_rows(rowv, base, reps):
    return jnp.concatenate([rowv[:, base + h * HEAD_DIM:base + (h + 1) * HEAD_DIM]
                            for h in range(N_HEADS) for _ in range(reps)], axis=0)


def _paged_softmax_step(q8, k_refs, v_refs, valid, scale, m_s, l_s, acc_s):
    q8b = q8.astype(BF16)
    s = jnp.concatenate([_nt_dot(q8b, k_ref[...].astype(BF16)) for k_ref in k_refs], axis=1) * scale
    s = jnp.where(valid, s, NEG_BIG)
    m_old = m_s[...]
    m_new = jnp.maximum(m_old, jnp.max(s, axis=1, keepdims=True))
    pr = jnp.where(valid, jnp.exp(s - m_new), 0.0)
    alpha = jnp.exp(m_old - m_new)
    l_s[...] = alpha * l_s[...] + jnp.sum(pr, axis=1, keepdims=True)
    prb = pr.astype(BF16)
    n = k_refs[0].shape[0]
    pv = jnp.zeros(acc_s.shape, F32)
    for g, v_ref in enumerate(v_refs):
        pv = pv + jnp.dot(prb[:, g * n:(g + 1) * n], v_ref[...].astype(BF16), preferred_element_type=F32)
    acc_s[...] = alpha * acc_s[...] + pv
    m_s[...] = m_new


def _sample_diff_kernel(pt_ref, proj_ref, *refs, lam_init, group):
    k_refs, v_refs = refs[:group], refs[group:2 * group]
    lam_ref, gd_ref, y_ref, m_s, l_s, acc_s = refs[2 * group:]
    b = pl.program_id(0)
    p = pl.program_id(1)
    rowv = proj_ref[pl.ds(b, 1), :]
    q8 = _head_rows(rowv, C_CQ, 2)
    lane = lax.broadcasted_iota(I32, q8.shape, 1)
    odd = (lax.broadcasted_iota(I32, q8.shape, 0) & 1) == 1
    q8 = jnp.where((lane >= DIFF_DIM) == odd, q8, 0.0)
    scale = DIFF_DIM ** -0.5

    @pl.when(p == 0)
    def _():
        m_s[...] = jnp.sum(q8 * _head_rows(rowv, C_CK, 2), axis=1, keepdims=True) * scale
        l_s[...] = jnp.ones_like(l_s)
        acc_s[...] = _head_rows(rowv, C_CV, 2)

    n = group * k_refs[0].shape[0]
    valid = ((lax.broadcasted_iota(I32, (8, n), 1) & (N_HEADS - 1))
             == (lax.broadcasted_iota(I32, (8, n), 0) >> 1))
    _paged_softmax_step(q8, k_refs, v_refs, valid, scale, m_s, l_s, acc_s)

    @pl.when(p == pl.num_programs(1) - 1)
    def _():
        lam = _lambda(lam_ref, lam_init)
        a = acc_s[...] / l_s[...]
        for h in range(N_HEADS):
            o = a[2 * h:2 * h + 1] - lam * a[2 * h + 1:2 * h + 2]
            z = rowv[:, C_CZ + h * HEAD_DIM:C_CZ + (h + 1) * HEAD_DIM]
            y_ref[:, h * HEAD_DIM:(h + 1) * HEAD_DIM] = _diff_finish(o, gd_ref[...], z, lam_init).astype(BF16)


def _sample_diff(page_table, proj, cache_k, cache_v, layer, lam_p, g_diff, lam_init, group):
    nb, n_pages = page_table.shape
    grid_spec = pltpu.PrefetchScalarGridSpec(
        num_scalar_prefetch=1,
        grid=(nb, n_pages // group),
        in_specs=[pl.BlockSpec(proj.shape, lambda b, p, pt: (0, 0))]
                 + _paged_specs(cache_k, layer, group) + _paged_specs(cache_v, layer, group)
                 + [pl.BlockSpec((4, DIFF_DIM), lambda b, p, pt: (0, 0)),
                    pl.BlockSpec((1, HEAD_DIM), lambda b, p, pt: (0, 0))],
        out_specs=pl.BlockSpec((None, 1, GROUP_W), lambda b, p, pt: (b, 0, 0)),
        scratch_shapes=[pltpu.VMEM((8, 1), F32), pltpu.VMEM((8, 1), F32), pltpu.VMEM((8, HEAD_DIM), F32)],
    )
    return pl.pallas_call(
        functools.partial(_sample_diff_kernel, lam_init=lam_init, group=group),
        grid_spec=grid_spec,
        out_shape=jax.ShapeDtypeStruct((nb, 1, GROUP_W), BF16),
        compiler_params=_cparams(("parallel", "arbitrary")),
        name="sample_diff",
    )(page_table, proj, *([cache_k] * group), *([cache_v] * group), lam_p, g_diff)


def _sample_idx_heads(rowv):
    q = [rowv[:, C_IQ + h * IDX_DIM:C_IQ + (h + 1) * IDX_DIM] for h in range(N_IDX_HEADS)]
    w = [rowv[:, C_TAIL + T_IW + h:C_TAIL + T_IW + h + 1] * (N_IDX_HEADS ** -0.5) for h in range(N_IDX_HEADS)]
    return q, w


def _sample_score_kernel(pt_ref, proj_ref, *refs, topk, n_pages, group):
    ki_refs = refs[:group]
    sel_ref, keys_s = refs[group:]
    b = pl.program_id(0)
    p = pl.program_id(1)
    rowv = proj_ref[pl.ds(b, 1), :]
    qs, ws = _sample_idx_heads(rowv)
    rows, page = keys_s.shape

    @pl.when(p == 0)
    def _():
        keys_s[...] = jnp.full(keys_s.shape, INT_MIN, I32)

    qm = _pad_rows(jnp.concatenate(qs, axis=0), 8).astype(BF16)
    for g, ki_ref in enumerate(ki_refs):
        sh = jnp.dot(qm, ki_ref[...].astype(BF16), preferred_element_type=F32) * (IDX_DIM ** -0.5)
        sc = jnp.zeros((1, page), F32)
        for h in range(N_IDX_HEADS):
            sc = sc + ws[h] * jnp.maximum(sh[h:h + 1], 0.0)
        keys_s[pl.ds(p * group + g, 1), :] = _sort_key(sc)

    @pl.when(p == pl.num_programs(1) - 1)
    def _():
        kn = rowv[:, C_TAIL:C_TAIL + IDX_DIM]
        sn = jnp.zeros((1, 1), F32)
        for h in range(N_IDX_HEADS):
            dh = jnp.sum(qs[h] * kn, axis=1, keepdims=True)
            sn = sn + ws[h] * jnp.maximum(dh * (IDX_DIM ** -0.5), 0.0)
        lane = lax.broadcasted_iota(I32, (1, page), 1)
        keys_s[pl.ds(n_pages, 1), :] = jnp.where(lane == 0, _sort_key(sn), INT_MIN)

        keys = keys_s[...]

        def count(pred):
            hit = jnp.where(pred(keys), 1, 0).astype(I32)
            return jnp.sum(jnp.sum(hit, axis=0, keepdims=True), axis=1, keepdims=True)

        thr = _kth_largest(lambda cand: count(lambda kk: kk >= cand), (1, 1), topk)
        need = (topk - count(lambda kk: kk > thr)).astype(F32)
        eq = keys == thr
        eqb = jnp.where(eq, 1.0, 0.0).astype(BF16)
        r_i = lax.broadcasted_iota(I32, (page, page), 0)
        c_i = lax.broadcasted_iota(I32, (page, page), 1)
        within = jnp.dot(eqb, jnp.where(r_i <= c_i, 1.0, 0.0).astype(BF16), preferred_element_type=F32)
        row_tot = jnp.broadcast_to(within[:, page - 1:page], (rows, page)).astype(BF16)
        rr = lax.broadcasted_iota(I32, (rows, rows), 0)
        rc = lax.broadcasted_iota(I32, (rows, rows), 1)
        before = jnp.dot(jnp.where(rc < rr, 1.0, 0.0).astype(BF16), row_tot, preferred_element_type=F32)
        sel = (keys > thr) | (eq & (before + within <= need))
        e_k = lax.broadcasted_iota(I32, (page, page * N_HEADS), 0)
        e_j = lax.broadcasted_iota(I32, (page, page * N_HEADS), 1) // N_HEADS
        expand = jnp.where(e_k == e_j, 1.0, 0.0).astype(BF16)
        sel_ref[...] = jnp.dot(jnp.where(sel, 1.0, 0.0).astype(BF16), expand, preferred_element_type=F32)


def _sample_select(page_table, proj, cache_kidx_t, layer, topk, group):
    nb, n_pages = page_table.shape
    page = cache_kidx_t.shape[3]
    rows = -(-(n_pages + 1) // LANES) * LANES
    grid_spec = pltpu.PrefetchScalarGridSpec(
        num_scalar_prefetch=1,
        grid=(nb, n_pages // group),
        in_specs=[pl.BlockSpec(proj.shape, lambda b, p, pt: (0, 0))] + _paged_specs(cache_kidx_t, layer, group),
        out_specs=pl.BlockSpec((None, rows, page * N_HEADS), lambda b, p, pt: (b, 0, 0)),
        scratch_shapes=[pltpu.VMEM((rows, page), I32)],
    )
    return pl.pallas_call(
        functools.partial(_sample_score_kernel, topk=topk, n_pages=n_pages, group=group),
        grid_spec=grid_spec,
        out_shape=jax.ShapeDtypeStruct((nb, rows, page * N_HEADS), F32),
        compiler_params=_cparams(("parallel", "arbitrary")),
        name="sample_select",
    )(page_table, proj, *([cache_kidx_t] * group))


def _sample_dsa_kernel(pt_ref, proj_ref, *refs, n_pages, group):
    k_refs, v_refs = refs[:group], refs[group:2 * group]
    sel_ref, y_ref, m_s, l_s, acc_s = refs[2 * group:]
    b = pl.program_id(0)
    p = pl.program_id(1)
    rowv = proj_ref[pl.ds(b, 1), :]
    q8 = _pad_rows(_head_rows(rowv, C_DQ, 1), 8)
    scale = HEAD_DIM ** -0.5

    @pl.when(p == 0)
    def _():
        own = sel_ref[n_pages:n_pages + 1, 0:1] > 0.5
        s_own = jnp.sum(q8 * _pad_rows(_head_rows(rowv, C_DK, 1), 8), axis=1, keepdims=True) * scale
        m_s[...] = jnp.where(own, s_own, NEG_BIG)
        l_s[...] = jnp.where(own, jnp.ones_like(l_s), 0.0)
        acc_s[...] = jnp.where(own, _pad_rows(_head_rows(rowv, C_DV, 1), 8), 0.0)

    slab = k_refs[0].shape[0]
    picked = jnp.concatenate([sel_ref[pl.ds(p * group + g, 1), :] for g in range(group)], axis=1) > 0.5
    valid = ((lax.broadcasted_iota(I32, (8, group * slab), 1) & (N_HEADS - 1))
             == lax.broadcasted_iota(I32, (8, group * slab), 0)) & picked
    _paged_softmax_step(q8, k_refs, v_refs, valid, scale, m_s, l_s, acc_s)

    @pl.when(p == pl.num_programs(1) - 1)
    def _():
        for h in range(N_HEADS):
            o = acc_s[h:h + 1, :] / l_s[h:h + 1, :]
            z = rowv[:, C_DZ + h * HEAD_DIM:C_DZ + (h + 1) * HEAD_DIM]
            y_ref[:, h * HEAD_DIM:(h + 1) * HEAD_DIM] = (o * _silu(z)).astype(BF16)


def _sample_dsa(page_table, proj, cache_k, cache_v, layer, sel, group):
    nb, n_pages = page_table.shape
    per_b = lambda r, c: pl.BlockSpec((None, r, c), lambda b, p, pt: (b, 0, 0))
    grid_spec = pltpu.PrefetchScalarGridSpec(
        num_scalar_prefetch=1,
        grid=(nb, n_pages // group),
        in_specs=[pl.BlockSpec(proj.shape, lambda b, p, pt: (0, 0))]
                 + _paged_specs(cache_k, layer, group) + _paged_specs(cache_v, layer, group)
                 + [per_b(sel.shape[1], sel.shape[2])],
        out_specs=per_b(1, GROUP_W),
        scratch_shapes=[pltpu.VMEM((8, 1), F32), pltpu.VMEM((8, 1), F32), pltpu.VMEM((8, HEAD_DIM), F32)],
    )
    return pl.pallas_call(
        functools.partial(_sample_dsa_kernel, n_pages=n_pages, group=group),
        grid_spec=grid_spec,
        out_shape=jax.ShapeDtypeStruct((nb, 1, GROUP_W), BF16),
        compiler_params=_cparams(("parallel", "arbitrary")),
        name="sample_dsa",
    )(page_table, proj, *([cache_k] * group), *([cache_v] * group), sel)


def _reorder_w_in(w_in):
    depth, d, _ = w_in.shape
    pad = jnp.zeros((depth, d, PROJ_PAD - PROJ_USED), w_in.dtype)
    w = jnp.concatenate([w_in[..., :ORIG_GATES], w_in[..., ORIG_GATES + 2 * N_HEADS:],
                         w_in[..., ORIG_GATES:ORIG_GATES + 2 * N_HEADS], pad], axis=-1)
    return w.astype(BF16)


def _prompt_layer(x, layer, lw, final, g_final):
    t = x.shape[0]
    proj = _inproj(x, lw["g_norm"], lw["w_in"], tm=min(1024, t))
    gates = proj[:, C_TAIL + T_AI:C_TAIL + T_AI + 2 * N_HEADS]
    lam_init = 0.8 - 0.6 * math.exp(-0.3 * layer)
    L = min(256, t)
    ya, c1, n1, m1 = _mlstm_prompt(proj, gates, gates.T, lw["b_igate"], lw["b_fgate"], lw["g_mlstm"], L)
    yb = _gmlp_prompt(proj, lw["g_gmlp"], lw["w_spatial"], lw["b_spatial"].T, tr=min(512, t))
    yc = _diff_prompt(proj, lw["lam_p"], lw["g_diff"], lam_init, tq=min(512, t), tk=min(512, t))
    kidx = proj[:, C_TAIL:C_TAIL + IDX_DIM]
    kidx_t4 = jnp.tile(kidx.T.astype(BF16), (N_IDX_HEADS, 1))
    yd = _dsa_prompt(proj, kidx_t4, tq=min(256, t), rg=64)
    x_new = _outproj(x, (ya, yb, yc, yd), lw["w_out"], g_final, tm=min(256, t), final=final)
    heads = lambda c0: proj[:, c0:c0 + GROUP_W].reshape(1, t, N_HEADS, HEAD_DIM)
    new = {"mlstm_c": c1[None], "mlstm_n": n1[:, 0][None], "mlstm_m": m1[:, 0, 0][None],
           "diff_k": heads(C_CK), "diff_v": heads(C_CV), "dsa_k": heads(C_DK), "dsa_v": heads(C_DV),
           "dsa_kidx": kidx[None]}
    return x_new, new


def _sample_layer(x, layer, lw, past, page_table, caches, final, g_final):
    nb = x.shape[0]
    proj = _inproj(x, lw["g_norm"], lw["w_in"], tm=nb)
    lam_init = 0.8 - 0.6 * math.exp(-0.3 * layer)
    ya, yb, gv, c1, n1, m1 = _sample_small(
        proj, past["c"], past["n"][:, :, None, :], jnp.broadcast_to(past["m"][:, :, None, None], (nb, N_HEADS, 1, LANES)),
        lw["b_igate"], lw["b_fgate"], lw["g_mlstm"], lw["g_gmlp"],
        lw["w_spatial"][:, 0, 0], lw["b_spatial"][:, 0])
    n_pages = page_table.shape[1]
    page = caches["dsa_kidx_t"].shape[3]
    group = math.gcd(n_pages, SAMPLE_PAGE_GROUP)
    yc = _sample_diff(page_table, proj, caches["diff_k"], caches["diff_v"], layer, lw["lam_p"], lw["g_diff"],
                      lam_init, group)
    topk = min(TOPK_MAX, (n_pages * page + 1) // 4)
    sel = _sample_select(page_table, proj, caches["dsa_kidx_t"], layer, topk, math.gcd(n_pages, 2 * SAMPLE_PAGE_GROUP))
    yd = _sample_dsa(page_table, proj, caches["dsa_k"], caches["dsa_v"], layer, sel, group)
    flat = lambda y: y.reshape(nb, GROUP_W)
    x_new = _outproj(x, (flat(ya), flat(yb), flat(yc), flat(yd)), lw["w_out"], g_final, tm=nb, final=final)
    heads = lambda c0: proj[:, c0:c0 + GROUP_W].reshape(nb, 1, N_HEADS, HEAD_DIM)
    new = {"mlstm_c": c1, "mlstm_n": n1[:, :, 0], "mlstm_m": m1[:, :, 0, 0],
           "diff_k": heads(C_CK), "diff_v": heads(C_CV), "dsa_k": heads(C_DK), "dsa_v": heads(C_DV),
           "dsa_kidx": proj[:, C_TAIL:C_TAIL + IDX_DIM].reshape(nb, 1, IDX_DIM), "gmlp_v": gv}
    return x_new, new


def kernel(x_prompt, x_sample, state_mlstm_c, state_mlstm_n, state_mlstm_m, cache_diff_k, cache_diff_v, cache_dsa_k, cache_dsa_v, cache_dsa_kidx, page_table, g_norm, w_in, b_igate, b_fgate, g_mlstm, g_gmlp, w_spatial, b_spatial, lambda_q1, lambda_k1, lambda_q2, lambda_k2, g_diff, w_out, g_final):
    depth = w_in.shape[0]
    assert x_prompt.shape[0] == 1 and x_sample.shape[1] == 1
    assert w_in.shape[2] == PROJ_USED
    xp = x_prompt[0]
    xs = x_sample[:, 0]
    w_in_r = _reorder_w_in(w_in)
    w_out_b = w_out.astype(BF16)
    n_phys, page = cache_diff_k.shape[1], cache_diff_k.shape[2]
    paged = lambda c: c.reshape(depth, n_phys, page * N_HEADS, HEAD_DIM)
    caches = {"diff_k": paged(cache_diff_k), "diff_v": paged(cache_diff_v),
              "dsa_k": paged(cache_dsa_k), "dsa_v": paged(cache_dsa_v),
              "dsa_kidx_t": jnp.swapaxes(cache_dsa_kidx, 2, 3)}
    g_final2 = g_final[None]
    pn, sn = [], []
    for l in range(depth):
        lw = {"g_norm": g_norm[l][None], "w_in": w_in_r[l], "b_igate": b_igate[l], "b_fgate": b_fgate[l],
              "g_mlstm": g_mlstm[l][None], "g_gmlp": g_gmlp[l][None], "w_spatial": w_spatial[l],
              "b_spatial": b_spatial[l], "g_diff": g_diff[l][None], "w_out": w_out_b[l],
              "lam_p": jnp.stack([lambda_q1[l], lambda_k1[l], lambda_q2[l], lambda_k2[l]])}
        final = l == depth - 1
        xp, p_new = _prompt_layer(xp, l, lw, final, g_final2)
        past = {"c": state_mlstm_c[l], "n": state_mlstm_n[l], "m": state_mlstm_m[l]}
        xs, s_new = _sample_layer(xs, l, lw, past, page_table, caches, final, g_final2)
        pn.append(p_new)
        sn.append(s_new)
    st = lambda lst, name: jnp.stack([d[name] for d in lst])
    names = ("mlstm_c", "mlstm_n", "mlstm_m", "diff_k", "diff_v", "dsa_k", "dsa_v", "dsa_kidx")
    return ((xp[None], xs[:, None]) + tuple(st(pn, n) for n in names) + tuple(st(sn, n) for n in names)
            + (st(sn, "gmlp_v"),))
```

```python
import functools
import math

import jax
import jax.numpy as jnp
from jax import lax
from jax.experimental import pallas as pl
from jax.experimental.pallas import tpu as pltpu

F32 = jnp.float32
BF16 = jnp.bfloat16
I32 = jnp.int32

N_HEADS = 4
HEAD_DIM = 128
GROUP_W = N_HEADS * HEAD_DIM
DIFF_DIM = HEAD_DIM // 2
N_IDX_HEADS = 4
IDX_DIM = 64
TOPK_MAX = 256
EPS = 1e-6
LANES = 128
NEG_BIG = -1e30
M_INIT = -5e29
LOG2E = 1.4426950408889634
COUNT_ROWS = 32
INT_MIN = -(2 ** 31)
KEY_POS_INF = 0x7F800000
KEY_NEG_INF = INT_MIN + 0x007FFFFF
KEY_LOWEST_FINITE = INT_MIN + 0x00800000

C_AQ, C_AK, C_AV, C_AO, C_AZ = 0, 512, 1024, 1536, 2048
C_BU, C_BV, C_BZ = 2560, 3072, 3584
C_CQ, C_CK, C_CV, C_CZ = 4096, 4608, 5120, 5632
C_DQ, C_DK, C_DV, C_DZ = 6144, 6656, 7168, 7680
C_IQ = 8192
C_TAIL = 8448
T_IW, T_AI, T_AF = 64, 68, 72
PROJ_USED = 8524
PROJ_PAD = 8704
ORIG_GATES = 2560

VMEM_LIMIT = 48 * 1024 * 1024
SAMPLE_PAGE_GROUP = 8


def _cparams(sem):
    return pltpu.CompilerParams(dimension_semantics=sem, vmem_limit_bytes=VMEM_LIMIT)


def _silu(z):
    return z * jax.nn.sigmoid(z)


def _log_sigmoid(x):
    return jnp.minimum(x, 0.0) - jnp.log1p(jnp.exp(-jnp.abs(x)))


def _nt_dot(a, b):
    return lax.dot_general(a, b, (((1,), (1,)), ((), ())), preferred_element_type=F32)


def _tn_dot(a, b):
    return lax.dot_general(a, b, (((0,), (0,)), ((), ())), preferred_element_type=F32)


def _key_to_float(key):
    key = jnp.clip(key, KEY_NEG_INF, KEY_POS_INF)
    return lax.bitcast_convert_type(key ^ ((key >> 31) & 0x7FFFFFFF), F32)


def _kth_largest(count_ge, shape, kk):
    key = jnp.where(count_ge(jnp.zeros(shape, F32)) >= kk, 0, INT_MIN).astype(I32)

    def bit_step(i, key):
        cand = key | (jnp.int32(1) << (30 - i))
        return jnp.where(count_ge(_key_to_float(cand)) >= kk, cand, key)

    return _key_to_float(jnp.maximum(lax.fori_loop(0, 31, bit_step, key), KEY_LOWEST_FINITE))


def _inproj_kernel(x_ref, g_ref, w_ref, o_ref, xn_ref):
    @pl.when(pl.program_id(1) == 0)
    def _():
        x = x_ref[...]
        ms = jnp.mean(x * x, axis=-1, keepdims=True)
        xn_ref[...] = (x * lax.rsqrt(ms + EPS) * g_ref[...]).astype(BF16)

    o_ref[...] = jnp.dot(xn_ref[...], w_ref[...], preferred_element_type=F32)


def _inproj(x, g, w, tm):
    m, d = x.shape
    tn = 512
    return pl.pallas_call(
        _inproj_kernel,
        grid=(m // tm, PROJ_PAD // tn),
        in_specs=[pl.BlockSpec((tm, d), lambda i, j: (i, 0)),
                  pl.BlockSpec((1, d), lambda i, j: (0, 0)),
                  pl.BlockSpec((d, tn), lambda i, j: (0, j))],
        out_specs=pl.BlockSpec((tm, tn), lambda i, j: (i, j)),
        out_shape=jax.ShapeDtypeStruct((m, PROJ_PAD), F32),
        scratch_shapes=[pltpu.VMEM((tm, d), BF16)],
        compiler_params=_cparams(("parallel", "arbitrary")),
        name="inproj",
    )(x, g, w)


def _inproj_f32_kernel(x_ref, g_ref, w_ref, o_ref):
    x = x_ref[...]
    xn = x * lax.rsqrt(jnp.mean(x * x, axis=-1, keepdims=True) + EPS) * g_ref[...]
    o_ref[...] = jnp.dot(xn, w_ref[...], precision=lax.Precision.HIGHEST, preferred_element_type=F32)


def _inproj_f32(x, g, w, tn):
    m, d = x.shape
    n = w.shape[1]
    return pl.pallas_call(
        _inproj_f32_kernel,
        grid=(pl.cdiv(n, tn),),
        in_specs=[pl.BlockSpec((m, d), lambda j: (0, 0)),
                  pl.BlockSpec((1, d), lambda j: (0, 0)),
                  pl.BlockSpec((d, tn), lambda j: (0, j))],
        out_specs=pl.BlockSpec((m, tn), lambda j: (0, j)),
        out_shape=jax.ShapeDtypeStruct((m, n), F32),
        compiler_params=_cparams(("parallel",)),
        name="inproj_f32",
    )(x, g, w)


def _reorder_cols(a):
    lead = a.shape[:-1]
    pad = jnp.zeros(lead + (PROJ_PAD - PROJ_USED,), a.dtype)
    return jnp.concatenate([a[..., :ORIG_GATES], a[..., ORIG_GATES + 2 * N_HEADS:],
                            a[..., ORIG_GATES:ORIG_GATES + 2 * N_HEADS], pad], axis=-1)


def _outproj_kernel(x_ref, a_ref, b_ref, c_ref, d_ref, w_ref, gf_ref, o_ref, *, final):
    acc = x_ref[...]
    for i, m_ref in enumerate((a_ref, b_ref, c_ref, d_ref)):
        acc = acc + jnp.dot(m_ref[...], w_ref[i * GROUP_W:(i + 1) * GROUP_W, :], preferred_element_type=F32)
    if final:
        ms = jnp.mean(acc * acc, axis=-1, keepdims=True)
        acc = acc * lax.rsqrt(ms + EPS) * gf_ref[...]
    o_ref[...] = acc


def _outproj(x, mixes, w, gf, tm, final):
    m, d = x.shape
    row = lambda i: (i, 0)
    const = lambda i: (0, 0)
    return pl.pallas_call(
        functools.partial(_outproj_kernel, final=final),
        grid=(m // tm,),
        in_specs=[pl.BlockSpec((tm, d), row)] + [pl.BlockSpec((tm, GROUP_W), row)] * 4
                 + [pl.BlockSpec((4 * GROUP_W, d), const), pl.BlockSpec((1, d), const)],
        out_specs=pl.BlockSpec((tm, d), row),
        out_shape=jax.ShapeDtypeStruct((m, d), F32),
        compiler_params=_cparams(("parallel",)),
        name="outproj",
    )(x, *mixes, w, gf)


def _mlstm_kernel(q_ref, k_ref, v_ref, o_ref, z_ref, g_ref, gt_ref, bi_ref, bf_ref, gm_ref,
                  y_ref, c_ref, n_ref, m_ref, *, L):
    h = pl.program_id(0)

    @pl.when(pl.program_id(1) == 0)
    def _():
        c_ref[...] = jnp.zeros_like(c_ref)
        n_ref[...] = jnp.zeros_like(n_ref)
        m_ref[...] = jnp.zeros_like(m_ref)

    bi = bi_ref[h]
    bf = bf_ref[h]
    q = q_ref[...] * (HEAD_DIM ** -0.5)
    k = k_ref[...]
    qb = q.astype(BF16)
    kb = k.astype(BF16)
    vb = v_ref[...].astype(BF16)

    li_r = gt_ref[pl.ds(h, 1), :] + bi
    lf_r = _log_sigmoid(gt_ref[pl.ds(h + N_HEADS, 1), :] + bf)
    g = g_ref[...]
    lane8 = lax.broadcasted_iota(I32, g.shape, 1)
    li_c = jnp.sum(jnp.where(lane8 == h, g, 0.0), axis=1, keepdims=True) + bi
    lf_c = _log_sigmoid(jnp.sum(jnp.where(lane8 == h + N_HEADS, g, 0.0), axis=1, keepdims=True) + bf)

    row = lax.broadcasted_iota(I32, (L, L), 0)
    col = lax.broadcasted_iota(I32, (L, L), 1)
    causal = col <= row
    b_c = jnp.sum(jnp.where(causal, lf_r, 0.0), axis=1, keepdims=True)
    b_r = jnp.sum(jnp.where(row <= col, lf_c, 0.0), axis=0, keepdims=True)
    b_last = jnp.sum(lf_r, axis=1, keepdims=True)

    m0 = m_ref[:, 0:1]
    c0 = c_ref[...]
    n0 = n_ref[...]
    dmat = jnp.where(causal, b_c - b_r + li_r, -jnp.inf)
    inter = b_c + m0
    m_t = jnp.maximum(inter, jnp.max(dmat, axis=1, keepdims=True))
    w_inter = jnp.exp(inter - m_t)
    s = _nt_dot(qb, kb) * jnp.exp(dmat - m_t)
    num = (w_inter * jnp.dot(qb, c0.astype(BF16), preferred_element_type=F32)
           + jnp.dot(s.astype(BF16), vb, preferred_element_type=F32))
    nq = w_inter * jnp.sum(q * n0, axis=1, keepdims=True) + jnp.sum(s, axis=1, keepdims=True)
    den = jnp.maximum(jnp.abs(nq), jnp.exp(-m_t))
    hh = num / den

    g_r = b_last - b_r + li_r
    g_c = b_last - b_c + li_c
    m_new = jnp.maximum(b_last + m0, jnp.max(g_r, axis=1, keepdims=True))
    ws_c = jnp.exp(g_c - m_new)
    wc = jnp.exp(b_last + m0 - m_new)
    kw = k * ws_c
    c_ref[...] = wc * c0 + _tn_dot(kw.astype(BF16), vb)
    n_ref[...] = wc * n0 + jnp.sum(kw, axis=0, keepdims=True)
    m_ref[...] = jnp.broadcast_to(m_new, m_ref.shape)

    ha = jax.nn.sigmoid(o_ref[...]) * hh
    y = ha * lax.rsqrt(jnp.mean(ha * ha, axis=1, keepdims=True) + EPS) * gm_ref[...]
    y_ref[...] = (y * _silu(z_ref[...])).astype(BF16)


def _mlstm_prompt(proj, gates, gates_t, b_i, b_f, g_mlstm, L):
    t = proj.shape[0]
    hb = lambda base: (lambda h, c: (c, base // HEAD_DIM + h))
    smem = pl.BlockSpec(memory_space=pltpu.SMEM)
    return pl.pallas_call(
        functools.partial(_mlstm_kernel, L=L),
        grid=(N_HEADS, t // L),
        in_specs=[pl.BlockSpec((L, HEAD_DIM), hb(C_AQ)), pl.BlockSpec((L, HEAD_DIM), hb(C_AK)),
                  pl.BlockSpec((L, HEAD_DIM), hb(C_AV)), pl.BlockSpec((L, HEAD_DIM), hb(C_AO)),
                  pl.BlockSpec((L, HEAD_DIM), hb(C_AZ)),
                  pl.BlockSpec((L, 2 * N_HEADS), lambda h, c: (c, 0)),
                  pl.BlockSpec((2 * N_HEADS, L), lambda h, c: (0, c)),
                  smem, smem,
                  pl.BlockSpec((1, HEAD_DIM), lambda h, c: (0, h))],
        out_specs=[pl.BlockSpec((L, HEAD_DIM), lambda h, c: (c, h)),
                   pl.BlockSpec((None, HEAD_DIM, HEAD_DIM), lambda h, c: (h, 0, 0)),
                   pl.BlockSpec((None, 1, HEAD_DIM), lambda h, c: (h, 0, 0)),
                   pl.BlockSpec((None, 1, LANES), lambda h, c: (h, 0, 0))],
        out_shape=[jax.ShapeDtypeStruct((t, GROUP_W), BF16),
                   jax.ShapeDtypeStruct((N_HEADS, HEAD_DIM, HEAD_DIM), F32),
                   jax.ShapeDtypeStruct((N_HEADS, 1, HEAD_DIM), F32),
                   jax.ShapeDtypeStruct((N_HEADS, 1, LANES), F32)],
        compiler_params=_cparams(("parallel", "arbitrary")),
        name="mlstm_prompt",
    )(proj, proj, proj, proj, proj, gates, gates_t, b_i, b_f, g_mlstm)


def _gmlp_kernel(u_ref, v_ref, z_ref, gg_ref, w_ref, bs_ref, y_ref, *, C):
    u = jax.nn.gelu(u_ref[...])
    gv = jax.nn.gelu(v_ref[...])
    v = gv * lax.rsqrt(jnp.mean(gv * gv, axis=1, keepdims=True) + EPS) * gg_ref[...]
    sz = _silu(z_ref[...])
    row = lax.broadcasted_iota(I32, (C, C), 0)
    col = lax.broadcasted_iota(I32, (C, C), 1)
    for g in range(N_HEADS):
        w = jnp.where(col <= row, w_ref[g], 0.0).astype(BF16)
        bcol = bs_ref[:, g:g + 1]
        gs = slice(g * HEAD_DIM, (g + 1) * HEAD_DIM)
        for r in range(u.shape[0] // C):
            rs = slice(r * C, (r + 1) * C)
            mixed = jnp.dot(w, v[rs, gs].astype(BF16), preferred_element_type=F32) + bcol
            y_ref[rs, gs] = (u[rs, gs] * mixed * sz[rs, gs]).astype(BF16)


def _gmlp_prompt(proj, g_gmlp, w_spatial, b_spatial_t, tr):
    t = proj.shape[0]
    c = w_spatial.shape[-1]
    blk = lambda base: pl.BlockSpec((tr, GROUP_W), lambda i: (i, base // GROUP_W))
    return pl.pallas_call(
        functools.partial(_gmlp_kernel, C=c),
        grid=(t // tr,),
        in_specs=[blk(C_BU), blk(C_BV), blk(C_BZ),
                  pl.BlockSpec((1, GROUP_W), lambda i: (0, 0)),
                  pl.BlockSpec((N_HEADS, c, c), lambda i: (0, 0, 0)),
                  pl.BlockSpec((c, N_HEADS), lambda i: (0, 0))],
        out_specs=pl.BlockSpec((tr, GROUP_W), lambda i: (i, 0)),
        out_shape=jax.ShapeDtypeStruct((t, GROUP_W), BF16),
        compiler_params=_cparams(("parallel",)),
        name="gmlp_prompt",
    )(proj, proj, proj, g_gmlp, w_spatial, b_spatial_t)


def _lambda(lam_ref, lam_init):
    lp = lam_ref[...]
    s1 = jnp.sum(lp[0:1] * lp[1:2], axis=1, keepdims=True)
    s2 = jnp.sum(lp[2:3] * lp[3:4], axis=1, keepdims=True)
    return jnp.exp(s1) - jnp.exp(s2) + lam_init


def _diff_finish(o, gd, z, lam_init):
    y = o * lax.rsqrt(jnp.mean(o * o, axis=1, keepdims=True) + EPS) * gd
    return (y * (1.0 - lam_init)) * _silu(z)


def _tri_schedule(nq, ratio):
    qs = [qi for qi in range(nq) for _ in range(qi // ratio + 1)]
    ks = [ki for qi in range(nq) for ki in range(qi // ratio + 1)]
    return jnp.asarray(qs, I32), jnp.asarray(ks, I32)


def _softmax_streams_t(score_fns, vts, m_s, l_s, acc_s, s_s, p_s, bias_ref=None):
    n = len(score_fns)
    for i in range(n):
        s = score_fns[i]()
        s_s[i] = s if bias_ref is None else s + bias_ref[...]
    alphas = []
    for i in range(n):
        s = s_s[i]
        m_old = m_s[i]
        m_new = jnp.maximum(m_old, jnp.max(s, axis=0, keepdims=True))
        p = jnp.exp2(s - m_new)
        alpha = jnp.exp2(m_old - m_new)
        l_s[i] = alpha * l_s[i] + jnp.sum(p, axis=0, keepdims=True)
        p_s[i] = p.astype(BF16)
        m_s[i] = m_new
        alphas.append(alpha)
    for i in range(n):
        acc_s[i] = alphas[i] * acc_s[i] + jnp.dot(vts[i], p_s[i], preferred_element_type=F32)


def _diff_kernel(qi_ref, ki_ref, qt_ref, k_ref, vt_ref, zt_ref, lam_ref, gd_ref, yt_ref,
                 q_s, m_s, l_s, acc_s, s_s, p_s, bias_s, *, lam_init, tq, tk):
    step = pl.program_id(0)
    qi = qi_ref[step]
    ki = ki_ref[step]
    last = ki == qi // (tk // tq)

    @pl.when(ki == 0)
    def _():
        q = qt_ref[...] * (DIFF_DIM ** -0.5 * LOG2E)
        feat = lax.broadcasted_iota(I32, q.shape, 0) & (HEAD_DIM - 1)
        q_s[0] = jnp.where(feat < DIFF_DIM, q, 0.0).astype(BF16)
        q_s[1] = jnp.where(feat >= DIFF_DIM, q, 0.0).astype(BF16)
        m_s[...] = jnp.full_like(m_s, NEG_BIG)
        l_s[...] = jnp.zeros_like(l_s)
        acc_s[...] = jnp.zeros_like(acc_s)

    def tile(masked):
        if masked:
            krow = ki * tk + lax.broadcasted_iota(I32, (tk, tq), 0)
            qcol = qi * tq + lax.broadcasted_iota(I32, (tk, tq), 1)
            bias_s[...] = jnp.where(krow <= qcol, 0.0, NEG_BIG)

        def score(h, j):
            hs = slice(h * HEAD_DIM, (h + 1) * HEAD_DIM)
            return jnp.dot(k_ref[:, hs], q_s[j, hs, :], preferred_element_type=F32)

        streams = [(h, j) for h in range(N_HEADS) for j in range(2)]
        _softmax_streams_t([functools.partial(score, h, j) for h, j in streams],
                           [vt_ref[h * HEAD_DIM:(h + 1) * HEAD_DIM, :] for h, _ in streams],
                           m_s, l_s, acc_s, s_s, p_s, bias_s if masked else None)

    @pl.when(jnp.logical_not(last))
    def _():
        tile(False)

    @pl.when(last)
    def _():
        tile(True)
        lam = _lambda(lam_ref, lam_init)
        for h in range(N_HEADS):
            hs = slice(h * HEAD_DIM, (h + 1) * HEAD_DIM)
            o = acc_s[2 * h] / l_s[2 * h] - lam * (acc_s[2 * h + 1] / l_s[2 * h + 1])
            y = o * lax.rsqrt(jnp.mean(o * o, axis=0, keepdims=True) + EPS) * gd_ref[...]
            yt_ref[hs, :] = ((y * (1.0 - lam_init)) * _silu(zt_ref[hs, :])).astype(BF16)


def _diff_prompt(qt, k, vt, zt, lam_p, g_diff_col, lam_init, tq, tk):
    t = k.shape[0]
    qs, ks = _tri_schedule(t // tq, tk // tq)
    qblk = pl.BlockSpec((GROUP_W, tq), lambda s, qi, ki: (0, qi[s]))
    const = lambda s, qi, ki: (0, 0)
    grid_spec = pltpu.PrefetchScalarGridSpec(
        num_scalar_prefetch=2,
        grid=(qs.shape[0],),
        in_specs=[qblk,
                  pl.BlockSpec((tk, GROUP_W), lambda s, qi, ki: (ki[s], 0)),
                  pl.BlockSpec((GROUP_W, tk), lambda s, qi, ki: (0, ki[s])),
                  qblk,
                  pl.BlockSpec((4, DIFF_DIM), const),
                  pl.BlockSpec((HEAD_DIM, 1), const)],
        out_specs=qblk,
        scratch_shapes=[pltpu.VMEM((2, GROUP_W, tq), BF16),
                        pltpu.VMEM((2 * N_HEADS, 1, tq), F32), pltpu.VMEM((2 * N_HEADS, 1, tq), F32),
                        pltpu.VMEM((2 * N_HEADS, HEAD_DIM, tq), F32),
                        pltpu.VMEM((2 * N_HEADS, tk, tq), F32), pltpu.VMEM((2 * N_HEADS, tk, tq), BF16),
                        pltpu.VMEM((tk, tq), F32)],
    )
    return pl.pallas_call(
        functools.partial(_diff_kernel, lam_init=lam_init, tq=tq, tk=tk),
        grid_spec=grid_spec,
        out_shape=jax.ShapeDtypeStruct((GROUP_W, t), BF16),
        compiler_params=_cparams(("arbitrary",)),
        name="diff_prompt",
    )(qs, ks, qt, k, vt, zt, lam_p, g_diff_col)


def _dsa_kernel(qi_ref, ki_ref, qt_ref, k_ref, vt_ref, zt_ref, iqt_ref, wt_ref, kidx_ref, yt_ref,
                keys_s, ltri_s, thr_s, need_s, run_s, q_s, m_s, l_s, acc_s, s_s, p_s, bias_s, *, tq, tk, topk):
    step = pl.program_id(0)
    qi = qi_ref[step]
    ki = ki_ref[step]
    ratio = tk // tq

    @pl.when(step == 0)
    def _():
        r = lax.broadcasted_iota(I32, (tk, tk), 0)
        c = lax.broadcasted_iota(I32, (tk, tk), 1)
        ltri_s[...] = jnp.where(c <= r, 1.0, 0.0).astype(BF16)

    @pl.when(ki == 0)
    def _():
        w = wt_ref[...] * ((N_IDX_HEADS ** -0.5) * (IDX_DIM ** -0.5))
        qcol = qi * tq + lax.broadcasted_iota(I32, (tq, tq), 1)
        krow0 = lax.broadcasted_iota(I32, (tq, tq), 0)

        def score_chunk(c, carry):
            r0 = pl.multiple_of(c * tq, tq)
            kx = kidx_ref[pl.ds(r0, tq), :]
            sc = jnp.zeros((tq, tq), F32)
            for h in range(N_IDX_HEADS):
                sh = jnp.dot(kx, iqt_ref[h * IDX_DIM:(h + 1) * IDX_DIM, :], preferred_element_type=F32)
                sc = sc + w[h:h + 1, :] * jnp.maximum(sh, 0.0)
            keys_s[pl.ds(r0, tq), :] = jnp.where(c * tq + krow0 <= qcol, sc, -jnp.inf)
            return carry

        lax.fori_loop(0, ratio * (qi // ratio + 1), score_chunk, 0)

        def count(pred):
            def body(c, acc):
                kk = keys_s[pl.ds(pl.multiple_of(c * tq, tq), tq), :]
                hit = jnp.where(pred(kk), 1, 0).astype(I32)
                for j in range(tq // COUNT_ROWS):
                    acc = acc + hit[j * COUNT_ROWS:(j + 1) * COUNT_ROWS, :]
                return acc
            acc = lax.fori_loop(0, qi + 1, body, jnp.zeros((COUNT_ROWS, tq), I32))
            return jnp.sum(acc, axis=0, keepdims=True)

        thr = _kth_largest(lambda cand: count(lambda kk: kk >= cand), (1, tq), topk)
        thr_s[...] = thr
        need_s[...] = (topk - count(lambda kk: kk > thr)).astype(F32)
        run_s[...] = jnp.zeros_like(run_s)
        q_s[...] = (qt_ref[...] * (HEAD_DIM ** -0.5 * LOG2E)).astype(BF16)
        m_s[...] = jnp.full_like(m_s, M_INIT)
        l_s[...] = jnp.zeros_like(l_s)
        acc_s[...] = jnp.zeros_like(acc_s)

    key = keys_s[pl.ds(pl.multiple_of(ki * tk, tk), tk), :]
    thr = thr_s[...]
    eq = key == thr
    rank = run_s[...] + jnp.dot(ltri_s[...], jnp.where(eq, 1.0, 0.0).astype(BF16), preferred_element_type=F32)
    run_s[...] = rank[tk - 1:tk, :]
    bias_s[...] = jnp.where((key > thr) | (eq & (rank <= need_s[...])), 0.0, NEG_BIG)
    heads = [slice(h * HEAD_DIM, (h + 1) * HEAD_DIM) for h in range(N_HEADS)]
    _softmax_streams_t([lambda hs=hs: jnp.dot(k_ref[:, hs], q_s[hs, :], preferred_element_type=F32)
                        for hs in heads],
                       [vt_ref[hs, :] for hs in heads], m_s, l_s, acc_s, s_s, p_s, bias_s)

    @pl.when(ki == qi // ratio)
    def _():
        for h in range(N_HEADS):
            hs = slice(h * HEAD_DIM, (h + 1) * HEAD_DIM)
            yt_ref[hs, :] = ((acc_s[h] / l_s[h]) * _silu(zt_ref[hs, :])).astype(BF16)


def _dsa_prompt(qt, k, vt, zt, iqt, wt, kidx, tq, tk):
    t = k.shape[0]
    topk = min(TOPK_MAX, t // 4)
    qs, ks = _tri_schedule(t // tq, tk // tq)
    qblk = lambda rows: pl.BlockSpec((rows, tq), lambda s, qi, ki: (0, qi[s]))
    grid_spec = pltpu.PrefetchScalarGridSpec(
        num_scalar_prefetch=2,
        grid=(qs.shape[0],),
        in_specs=[qblk(GROUP_W),
                  pl.BlockSpec((tk, GROUP_W), lambda s, qi, ki: (ki[s], 0)),
                  pl.BlockSpec((GROUP_W, tk), lambda s, qi, ki: (0, ki[s])),
                  qblk(GROUP_W), qblk(N_IDX_HEADS * IDX_DIM), qblk(N_IDX_HEADS),
                  pl.BlockSpec((t, IDX_DIM), lambda s, qi, ki: (0, 0))],
        out_specs=qblk(GROUP_W),
        scratch_shapes=[pltpu.VMEM((t, tq), F32), pltpu.VMEM((tk, tk), BF16),
                        pltpu.VMEM((1, tq), F32), pltpu.VMEM((1, tq), F32), pltpu.VMEM((1, tq), F32),
                        pltpu.VMEM((GROUP_W, tq), BF16),
                        pltpu.VMEM((N_HEADS, 1, tq), F32), pltpu.VMEM((N_HEADS, 1, tq), F32),
                        pltpu.VMEM((N_HEADS, HEAD_DIM, tq), F32),
                        pltpu.VMEM((N_HEADS, tk, tq), F32), pltpu.VMEM((N_HEADS, tk, tq), BF16),
                        pltpu.VMEM((tk, tq), F32)],
    )
    return pl.pallas_call(
        functools.partial(_dsa_kernel, tq=tq, tk=tk, topk=topk),
        grid_spec=grid_spec,
        out_shape=jax.ShapeDtypeStruct((GROUP_W, t), BF16),
        compiler_params=_cparams(("arbitrary",)),
        name="dsa_prompt",
    )(qs, ks, qt, k, vt, zt, iqt, wt, kidx)


def _sample_small_kernel(proj_ref, c0_ref, n0_ref, m0_ref, bi_ref, bf_ref, gm_ref, gg_ref, w00_ref, b0_ref,
                         ya_ref, yb_ref, gv_ref, c1_ref, n1_ref, m1_ref):
    b = pl.program_id(0)
    rowv = proj_ref[pl.ds(b, 1), :]
    eye = lax.broadcasted_iota(I32, (HEAD_DIM, HEAD_DIM), 0) == lax.broadcasted_iota(I32, (HEAD_DIM, HEAD_DIM), 1)
    for h in range(N_HEADS):
        hs = lambda base: slice(base + h * HEAD_DIM, base + (h + 1) * HEAD_DIM)
        q = rowv[:, hs(C_AQ)] * (HEAD_DIM ** -0.5)
        k = rowv[:, hs(C_AK)]
        v = rowv[:, hs(C_AV)]
        li = rowv[:, C_TAIL + T_AI + h:C_TAIL + T_AI + h + 1] + bi_ref[h]
        lf = _log_sigmoid(rowv[:, C_TAIL + T_AF + h:C_TAIL + T_AF + h + 1] + bf_ref[h])
        c0 = c0_ref[h]
        n0 = n0_ref[h]
        m0 = m0_ref[h][:, 0:1]
        inter = lf + m0
        m_t = jnp.maximum(inter, li)
        w_inter = jnp.exp(inter - m_t)
        w_new = jnp.exp(li - m_t)
        s = jnp.sum(q * k, axis=1, keepdims=True) * w_new
        qc = jnp.dot(q, c0, precision=lax.Precision.HIGHEST, preferred_element_type=F32)
        num = w_inter * qc + s * v
        nq = w_inter * jnp.sum(q * n0, axis=1, keepdims=True) + s
        den = jnp.maximum(jnp.abs(nq), jnp.exp(-m_t))
        hh = num / den
        k_col = jnp.sum(jnp.where(eye, k, 0.0), axis=1, keepdims=True)
        c1_ref[h] = w_inter * c0 + w_new * (k_col * v)
        n1_ref[h] = w_inter * n0 + w_new * k
        m1_ref[h] = jnp.broadcast_to(m_t, (1, LANES))
        ha = jax.nn.sigmoid(rowv[:, hs(C_AO)]) * hh
        y = ha * lax.rsqrt(jnp.mean(ha * ha, axis=1, keepdims=True) + EPS) * gm_ref[:, h * HEAD_DIM:(h + 1) * HEAD_DIM]
        ya_ref[:, h * HEAD_DIM:(h + 1) * HEAD_DIM] = (y * _silu(rowv[:, hs(C_AZ)])).astype(BF16)

    u = jax.nn.gelu(rowv[:, C_BU:C_BU + GROUP_W])
    gv = jax.nn.gelu(rowv[:, C_BV:C_BV + GROUP_W])
    vv = gv * lax.rsqrt(jnp.mean(gv * gv, axis=1, keepdims=True) + EPS) * gg_ref[...]
    gv_ref[...] = vv
    sz = _silu(rowv[:, C_BZ:C_BZ + GROUP_W])
    for g in range(N_HEADS):
        gs = slice(g * HEAD_DIM, (g + 1) * HEAD_DIM)
        mixed = w00_ref[g] * vv[:, gs] + b0_ref[g]
        yb_ref[:, gs] = (u[:, gs] * mixed * sz[:, gs]).astype(BF16)


def _sample_small(proj, c0, n0, m0, b_i, b_f, g_mlstm, g_gmlp, w00, b0):
    nb = proj.shape[0]
    smem = pl.BlockSpec(memory_space=pltpu.SMEM)
    per_b = lambda *tail: pl.BlockSpec((None,) + tail, lambda b: (b,) + (0,) * len(tail))
    row_out = lambda dt: jax.ShapeDtypeStruct((nb, 1, GROUP_W), dt)
    return pl.pallas_call(
        _sample_small_kernel,
        grid=(nb,),
        in_specs=[pl.BlockSpec(proj.shape, lambda b: (0, 0)),
                  per_b(N_HEADS, HEAD_DIM, HEAD_DIM), per_b(N_HEADS, 1, HEAD_DIM), per_b(N_HEADS, 1, LANES),
                  smem, smem,
                  pl.BlockSpec((1, GROUP_W), lambda b: (0, 0)), pl.BlockSpec((1, GROUP_W), lambda b: (0, 0)),
                  smem, smem],
        out_specs=[per_b(1, GROUP_W), per_b(1, GROUP_W), per_b(1, GROUP_W),
                   per_b(N_HEADS, HEAD_DIM, HEAD_DIM), per_b(N_HEADS, 1, HEAD_DIM), per_b(N_HEADS, 1, LANES)],
        out_shape=[row_out(BF16), row_out(BF16), row_out(F32),
                   jax.ShapeDtypeStruct((nb, N_HEADS, HEAD_DIM, HEAD_DIM), F32),
                   jax.ShapeDtypeStruct((nb, N_HEADS, 1, HEAD_DIM), F32),
                   jax.ShapeDtypeStruct((nb, N_HEADS, 1, LANES), F32)],
        compiler_params=_cparams(("parallel",)),
        name="sample_small",
    )(proj, c0, n0, m0, b_i, b_f, g_mlstm, g_gmlp, w00, b0)


def _pad_rows(x, n):
    return jnp.concatenate([x, jnp.zeros((n - x.shape[0], x.shape[1]), x.dtype)], axis=0)


def _paged_specs(cache, layer, group):
    rows, lanes = cache.shape[2], cache.shape[3]
    return [pl.BlockSpec((None, None, rows, lanes), functools.partial(
        lambda b, p, pt, g: (layer, pt[b, p * group + g], 0, 0), g=g)) for g in range(group)]


def _head_rows(rowv, base, reps):
    return jnp.concatenate([rowv[:, base + h * HEAD_DIM:base + (h + 1) * HEAD_DIM]
                            for h in range(N_HEADS) for _ in range(reps)], axis=0)


def _paged_softmax_step(q8, k_refs, v_refs, valid, scale, m_s, l_s, acc_s):
    q8b = q8.astype(BF16)
    s = jnp.concatenate([_nt_dot(q8b, k_ref[...].astype(BF16)) for k_ref in k_refs], axis=1) * scale
    s = jnp.where(valid, s, NEG_BIG)
    m_old = m_s[...]
    m_new = jnp.maximum(m_old, jnp.max(s, axis=1, keepdims=True))
    pr = jnp.where(valid, jnp.exp(s - m_new), 0.0)
    alpha = jnp.exp(m_old - m_new)
    l_s[...] = alpha * l_s[...] + jnp.sum(pr, axis=1, keepdims=True)
    prb = pr.astype(BF16)
    n = k_refs[0].shape[0]
    pv = jnp.zeros(acc_s.shape, F32)
    for g, v_ref in enumerate(v_refs):
        pv = pv + jnp.dot(prb[:, g * n:(g + 1) * n], v_ref[...].astype(BF16), preferred_element_type=F32)
    acc_s[...] = alpha * acc_s[...] + pv
    m_s[...] = m_new


def _sample_diff_kernel(pt_ref, proj_ref, *refs, lam_init, group):
    k_refs, v_refs = refs[:group], refs[group:2 * group]
    lam_ref, gd_ref, y_ref, m_s, l_s, acc_s = refs[2 * group:]
    b = pl.program_id(0)
    p = pl.program_id(1)
    rowv = proj_ref[pl.ds(b, 1), :]
    q8 = _head_rows(rowv, C_CQ, 2)
    lane = lax.broadcasted_iota(I32, q8.shape, 1)
    odd = (lax.broadcasted_iota(I32, q8.shape, 0) & 1) == 1
    q8 = jnp.where((lane >= DIFF_DIM) == odd, q8, 0.0)
    scale = DIFF_DIM ** -0.5

    @pl.when(p == 0)
    def _():
        m_s[...] = jnp.sum(q8 * _head_rows(rowv, C_CK, 2), axis=1, keepdims=True) * scale
        l_s[...] = jnp.ones_like(l_s)
        acc_s[...] = _head_rows(rowv, C_CV, 2)

    n = group * k_refs[0].shape[0]
    valid = ((lax.broadcasted_iota(I32, (8, n), 1) & (N_HEADS - 1))
             == (lax.broadcasted_iota(I32, (8, n), 0) >> 1))
    _paged_softmax_step(q8, k_refs, v_refs, valid, scale, m_s, l_s, acc_s)

    @pl.when(p == pl.num_programs(1) - 1)
    def _():
        lam = _lambda(lam_ref, lam_init)
        a = acc_s[...] / l_s[...]
        for h in range(N_HEADS):
            o = a[2 * h:2 * h + 1] - lam * a[2 * h + 1:2 * h + 2]
            z = rowv[:, C_CZ + h * HEAD_DIM:C_CZ + (h + 1) * HEAD_DIM]
            y_ref[:, h * HEAD_DIM:(h + 1) * HEAD_DIM] = _diff_finish(o, gd_ref[...], z, lam_init).astype(BF16)


def _sample_diff(page_table, proj, cache_k, cache_v, layer, lam_p, g_diff, lam_init, group):
    nb, n_pages = page_table.shape
    grid_spec = pltpu.PrefetchScalarGridSpec(
        num_scalar_prefetch=1,
        grid=(nb, n_pages // group),
        in_specs=[pl.BlockSpec(proj.shape, lambda b, p, pt: (0, 0))]
                 + _paged_specs(cache_k, layer, group) + _paged_specs(cache_v, layer, group)
                 + [pl.BlockSpec((4, DIFF_DIM), lambda b, p, pt: (0, 0)),
                    pl.BlockSpec((1, HEAD_DIM), lambda b, p, pt: (0, 0))],
        out_specs=pl.BlockSpec((None, 1, GROUP_W), lambda b, p, pt: (b, 0, 0)),
        scratch_shapes=[pltpu.VMEM((8, 1), F32), pltpu.VMEM((8, 1), F32), pltpu.VMEM((8, HEAD_DIM), F32)],
    )
    return pl.pallas_call(
        functools.partial(_sample_diff_kernel, lam_init=lam_init, group=group),
        grid_spec=grid_spec,
        out_shape=jax.ShapeDtypeStruct((nb, 1, GROUP_W), BF16),
        compiler_params=_cparams(("parallel", "arbitrary")),
        name="sample_diff",
    )(page_table, proj, *([cache_k] * group), *([cache_v] * group), lam_p, g_diff)


def _sample_idx_heads(rowv):
    q = [rowv[:, C_IQ + h * IDX_DIM:C_IQ + (h + 1) * IDX_DIM] for h in range(N_IDX_HEADS)]
    w = [rowv[:, C_TAIL + T_IW + h:C_TAIL + T_IW + h + 1] * (N_IDX_HEADS ** -0.5) for h in range(N_IDX_HEADS)]
    return q, w


def _sample_score_kernel(pt_ref, proj_ref, *refs, topk, n_pages, group):
    ki_refs = refs[:group]
    sel_ref, keys_s, thr_s, need_s = refs[group:]
    b = pl.program_id(0)
    p = pl.program_id(1)
    rowv = proj_ref[pl.ds(b, 1), :]
    qs, ws = _sample_idx_heads(rowv)
    nb, rows, page = keys_s.shape

    @pl.when(p == 0)
    def _():
        keys_s[b] = jnp.full((rows, page), -jnp.inf, F32)

    qm = _pad_rows(jnp.concatenate(qs, axis=0), 8).astype(BF16)
    for g, ki_ref in enumerate(ki_refs):
        sh = jnp.dot(qm, ki_ref[...].astype(BF16), preferred_element_type=F32) * (IDX_DIM ** -0.5)
        sc = jnp.zeros((1, page), F32)
        for h in range(N_IDX_HEADS):
            sc = sc + ws[h] * jnp.maximum(sh[h:h + 1], 0.0)
        keys_s[b, pl.ds(p * group + g, 1), :] = sc

    last_page = p == pl.num_programs(1) - 1

    @pl.when(last_page)
    def _():
        kn = rowv[:, C_TAIL:C_TAIL + IDX_DIM]
        sn = jnp.zeros((1, 1), F32)
        for h in range(N_IDX_HEADS):
            dh = jnp.sum(qs[h] * kn, axis=1, keepdims=True)
            sn = sn + ws[h] * jnp.maximum(dh * (IDX_DIM ** -0.5), 0.0)
        lane = lax.broadcasted_iota(I32, (1, page), 1)
        keys_s[b, pl.ds(n_pages, 1), :] = jnp.where(lane == 0, sn, -jnp.inf)

    @pl.when(last_page & (b == nb - 1))
    def _():
        keys = keys_s[...]

        def count(pred):
            hit = jnp.where(pred(keys), 1, 0).astype(I32)
            return jnp.sum(jnp.sum(hit, axis=1, keepdims=True), axis=2, keepdims=True)

        thr_all = _kth_largest(lambda cand: count(lambda kk: kk >= cand), (nb, 1, 1), topk)
        thr_s[...] = jnp.broadcast_to(thr_all, thr_s.shape)
        need_s[...] = jnp.broadcast_to((topk - count(lambda kk: kk > thr_all)).astype(F32), need_s.shape)

        r_i = lax.broadcasted_iota(I32, (page, page), 0)
        c_i = lax.broadcasted_iota(I32, (page, page), 1)
        upper = jnp.where(r_i <= c_i, 1.0, 0.0).astype(BF16)
        rr = lax.broadcasted_iota(I32, (rows, rows), 0)
        rc = lax.broadcasted_iota(I32, (rows, rows), 1)
        strict_lower = jnp.where(rc < rr, 1.0, 0.0).astype(BF16)
        e_k = lax.broadcasted_iota(I32, (page, page * N_HEADS), 0)
        e_j = lax.broadcasted_iota(I32, (page, page * N_HEADS), 1) // N_HEADS
        expand = jnp.where(e_k == e_j, 1.0, 0.0).astype(BF16)

        def select_row(i, carry):
            kb = keys_s[i]
            thr = thr_s[i]
            eq = kb == thr
            within = jnp.dot(jnp.where(eq, 1.0, 0.0).astype(BF16), upper, preferred_element_type=F32)
            row_tot = jnp.broadcast_to(within[:, page - 1:page], (rows, page)).astype(BF16)
            before = jnp.dot(strict_lower, row_tot, preferred_element_type=F32)
            sel = (kb > thr) | (eq & (before + within <= need_s[i]))
            sel_ref[i] = jnp.dot(jnp.where(sel, 1.0, 0.0).astype(BF16), expand, preferred_element_type=F32)
            return carry

        lax.fori_loop(0, nb, select_row, 0)


def _sample_select(page_table, proj, cache_kidx_t, layer, topk, group):
    nb, n_pages = page_table.shape
    page = cache_kidx_t.shape[3]
    rows = -(-(n_pages + 1) // LANES) * LANES
    grid_spec = pltpu.PrefetchScalarGridSpec(
        num_scalar_prefetch=1,
        grid=(nb, n_pages // group),
        in_specs=[pl.BlockSpec(proj.shape, lambda b, p, pt: (0, 0))] + _paged_specs(cache_kidx_t, layer, group),
        out_specs=pl.BlockSpec((nb, rows, page * N_HEADS), lambda b, p, pt: (0, 0, 0)),
        scratch_shapes=[pltpu.VMEM((nb, rows, page), F32), pltpu.VMEM((nb, 1, LANES), F32),
                        pltpu.VMEM((nb, 1, LANES), F32)],
    )
    return pl.pallas_call(
        functools.partial(_sample_score_kernel, topk=topk, n_pages=n_pages, group=group),
        grid_spec=grid_spec,
        out_shape=jax.ShapeDtypeStruct((nb, rows, page * N_HEADS), F32),
        compiler_params=_cparams(("arbitrary", "arbitrary")),
        name="sample_select",
    )(page_table, proj, *([cache_kidx_t] * group))


def _sample_dsa_kernel(pt_ref, proj_ref, *refs, n_pages, group):
    k_refs, v_refs = refs[:group], refs[group:2 * group]
    sel_ref, y_ref, m_s, l_s, acc_s = refs[2 * group:]
    b = pl.program_id(0)
    p = pl.program_id(1)
    rowv = proj_ref[pl.ds(b, 1), :]
    q8 = _pad_rows(_head_rows(rowv, C_DQ, 1), 8)
    scale = HEAD_DIM ** -0.5

    @pl.when(p == 0)
    def _():
        own = sel_ref[n_pages:n_pages + 1, 0:1] > 0.5
        s_own = jnp.sum(q8 * _pad_rows(_head_rows(rowv, C_DK, 1), 8), axis=1, keepdims=True) * scale
        m_s[...] = jnp.where(own, s_own, NEG_BIG)
        l_s[...] = jnp.where(own, jnp.ones_like(l_s), 0.0)
        acc_s[...] = jnp.where(own, _pad_rows(_head_rows(rowv, C_DV, 1), 8), 0.0)

    slab = k_refs[0].shape[0]
    picked = jnp.concatenate([sel_ref[pl.ds(p * group + g, 1), :] for g in range(group)], axis=1) > 0.5
    valid = ((lax.broadcasted_iota(I32, (8, group * slab), 1) & (N_HEADS - 1))
             == lax.broadcasted_iota(I32, (8, group * slab), 0)) & picked
    _paged_softmax_step(q8, k_refs, v_refs, valid, scale, m_s, l_s, acc_s)

    @pl.when(p == pl.num_programs(1) - 1)
    def _():
        for h in range(N_HEADS):
            o = acc_s[h:h + 1, :] / l_s[h:h + 1, :]
            z = rowv[:, C_DZ + h * HEAD_DIM:C_DZ + (h + 1) * HEAD_DIM]
            y_ref[:, h * HEAD_DIM:(h + 1) * HEAD_DIM] = (o * _silu(z)).astype(BF16)


def _sample_dsa(page_table, proj, cache_k, cache_v, layer, sel, group):
    nb, n_pages = page_table.shape
    per_b = lambda r, c: pl.BlockSpec((None, r, c), lambda b, p, pt: (b, 0, 0))
    grid_spec = pltpu.PrefetchScalarGridSpec(
        num_scalar_prefetch=1,
        grid=(nb, n_pages // group),
        in_specs=[pl.BlockSpec(proj.shape, lambda b, p, pt: (0, 0))]
                 + _paged_specs(cache_k, layer, group) + _paged_specs(cache_v, layer, group)
                 + [per_b(sel.shape[1], sel.shape[2])],
        out_specs=per_b(1, GROUP_W),
        scratch_shapes=[pltpu.VMEM((8, 1), F32), pltpu.VMEM((8, 1), F32), pltpu.VMEM((8, HEAD_DIM), F32)],
    )
    return pl.pallas_call(
        functools.partial(_sample_dsa_kernel, n_pages=n_pages, group=group),
        grid_spec=grid_spec,
        out_shape=jax.ShapeDtypeStruct((nb, 1, GROUP_W), BF16),
        compiler_params=_cparams(("parallel", "arbitrary")),
        name="sample_dsa",
    )(page_table, proj, *([cache_k] * group), *([cache_v] * group), sel)


def _reorder_w_in(w_in):
    return _reorder_cols(w_in).astype(BF16)


def _prompt_layer(x, layer, lw, final, g_final):
    t = x.shape[0]
    proj = _inproj(x, lw["g_norm"], lw["w_in"], tm=min(1024, t))
    gates = proj[:, C_TAIL + T_AI:C_TAIL + T_AI + 2 * N_HEADS]
    lam_init = 0.8 - 0.6 * math.exp(-0.3 * layer)
    L = min(256, t)
    ya, c1, n1, m1 = _mlstm_prompt(proj, gates, gates.T, lw["b_igate"], lw["b_fgate"], lw["g_mlstm"], L)
    yb = _gmlp_prompt(proj, lw["g_gmlp"], lw["w_spatial"], lw["b_spatial"].T, tr=min(512, t))
    cols = lambda c0, w=GROUP_W: proj[:, c0:c0 + w]
    tq, tk = min(256, t), min(512, t)
    yc_t = _diff_prompt(cols(C_CQ).T, cols(C_CK).astype(BF16), cols(C_CV).T.astype(BF16), cols(C_CZ).T,
                        lw["lam_p"], lw["g_diff"].T, lam_init, tq, tk)
    kidx = cols(C_TAIL, IDX_DIM)
    yd_t = _dsa_prompt(cols(C_DQ).T, cols(C_DK).astype(BF16), cols(C_DV).T.astype(BF16), cols(C_DZ).T,
                       cols(C_IQ, N_IDX_HEADS * IDX_DIM).T.astype(BF16), cols(C_TAIL + T_IW, N_IDX_HEADS).T,
                       kidx.astype(BF16), tq, tk)
    x_new = _outproj(x, (ya, yb, yc_t.T, yd_t.T), lw["w_out"], g_final, tm=min(256, t), final=final)
    heads = lambda c0: proj[:, c0:c0 + GROUP_W].reshape(1, t, N_HEADS, HEAD_DIM)
    new = {"mlstm_c": c1[None], "mlstm_n": n1[:, 0][None], "mlstm_m": m1[:, 0, 0][None],
           "diff_k": heads(C_CK), "diff_v": heads(C_CV), "dsa_k": heads(C_DK), "dsa_v": heads(C_DV),
           "dsa_kidx": kidx[None]}
    return x_new, new


def _sample_layer(x, layer, lw, past, page_table, caches, final, g_final):
    nb = x.shape[0]
    proj = _reorder_cols(_inproj_f32(x, lw["g_norm"], lw["w_in_f32"], tn=1024))
    lam_init = 0.8 - 0.6 * math.exp(-0.3 * layer)
    ya, yb, gv, c1, n1, m1 = _sample_small(
        proj, past["c"], past["n"][:, :, None, :], jnp.broadcast_to(past["m"][:, :, None, None], (nb, N_HEADS, 1, LANES)),
        lw["b_igate"], lw["b_fgate"], lw["g_mlstm"], lw["g_gmlp"],
        lw["w_spatial"][:, 0, 0], lw["b_spatial"][:, 0])
    n_pages = page_table.shape[1]
    page = caches["dsa_kidx_t"].shape[3]
    group = math.gcd(n_pages, SAMPLE_PAGE_GROUP)
    yc = _sample_diff(page_table, proj, caches["diff_k"], caches["diff_v"], layer, lw["lam_p"], lw["g_diff"],
                      lam_init, group)
    topk = min(TOPK_MAX, (n_pages * page + 1) // 4)
    sel = _sample_select(page_table, proj, caches["dsa_kidx_t"], layer, topk, math.gcd(n_pages, 2 * SAMPLE_PAGE_GROUP))
    yd = _sample_dsa(page_table, proj, caches["dsa_k"], caches["dsa_v"], layer, sel, group)
    flat = lambda y: y.reshape(nb, GROUP_W)
    x_new = _outproj(x, (flat(ya), flat(yb), flat(yc), flat(yd)), lw["w_out"], g_final, tm=nb, final=final)
    heads = lambda c0: proj[:, c0:c0 + GROUP_W].reshape(nb, 1, N_HEADS, HEAD_DIM)
    new = {"mlstm_c": c1, "mlstm_n": n1[:, :, 0], "mlstm_m": m1[:, :, 0, 0],
           "diff_k": heads(C_CK), "diff_v": heads(C_CV), "dsa_k": heads(C_DK), "dsa_v": heads(C_DV),
           "dsa_kidx": proj[:, C_TAIL:C_TAIL + IDX_DIM].reshape(nb, 1, IDX_DIM), "gmlp_v": gv}
    return x_new, new


def kernel(x_prompt, x_sample, state_mlstm_c, state_mlstm_n, state_mlstm_m, cache_diff_k, cache_diff_v, cache_dsa_k, cache_dsa_v, cache_dsa_kidx, page_table, g_norm, w_in, b_igate, b_fgate, g_mlstm, g_gmlp, w_spatial, b_spatial, lambda_q1, lambda_k1, lambda_q2, lambda_k2, g_diff, w_out, g_final):
    depth = w_in.shape[0]
    assert x_prompt.shape[0] == 1 and x_sample.shape[1] == 1
    assert w_in.shape[2] == PROJ_USED
    xp = x_prompt[0]
    xs = x_sample[:, 0]
    w_in_r = _reorder_w_in(w_in)
    w_out_b = w_out.astype(BF16)
    n_phys, page = cache_diff_k.shape[1], cache_diff_k.shape[2]
    paged = lambda c: c.reshape(depth, n_phys, page * N_HEADS, HEAD_DIM)
    caches = {"diff_k": paged(cache_diff_k), "diff_v": paged(cache_diff_v),
              "dsa_k": paged(cache_dsa_k), "dsa_v": paged(cache_dsa_v),
              "dsa_kidx_t": jnp.swapaxes(cache_dsa_kidx, 2, 3)}
    g_final2 = g_final[None]
    pn, sn = [], []
    for l in range(depth):
        lw = {"g_norm": g_norm[l][None], "w_in": w_in_r[l], "w_in_f32": w_in[l], "b_igate": b_igate[l], "b_fgate": b_fgate[l],
              "g_mlstm": g_mlstm[l][None], "g_gmlp": g_gmlp[l][None], "w_spatial": w_spatial[l],
              "b_spatial": b_spatial[l], "g_diff": g_diff[l][None], "w_out": w_out_b[l],
              "lam_p": jnp.stack([lambda_q1[l], lambda_k1[l], lambda_q2[l], lambda_k2[l]])}
        final = l == depth - 1
        xp, p_new = _prompt_layer(xp, l, lw, final, g_final2)
        past = {"c": state_mlstm_c[l], "n": state_mlstm_n[l], "m": state_mlstm_m[l]}
        xs, s_new = _sample_layer(xs, l, lw, past, page_table, caches, final, g_final2)
        pn.append(p_new)
        sn.append(s_new)
    st = lambda lst, name: jnp.stack([d[name] for d in lst])
    names = ("mlstm_c", "mlstm_n", "mlstm_m", "diff_k", "diff_v", "dsa_k", "dsa_v", "dsa_kidx")
    return ((xp[None], xs[:, None]) + tuple(st(pn, n) for n in names) + tuple(st(sn, n) for n in names)
            + (st(sn, "gmlp_v"),))
```
